```python
import math
import jax, jax.numpy as jnp
from jax import lax
import numpy as np

D_MODEL = 1024
BATCH = 4
SEQ = 4096
DEPTH = 4

GRID_W = 64
CTX_LEN = 256
N_MIXERS = 3
Q_BLOCK = 128
NORM_EPS = 1e-6
ROPE_THETA = 10000.0
NEG_INF = -1e30

A_HEADS = 16
A_KV_HEADS = 4
A_REP = A_HEADS // A_KV_HEADS
HEAD_DIM = 64
B_HEADS = 16
WIN_R = 8
WIN_C = 16
C_HEADS = 16
Q_LORA = 384
KV_LORA = 256
NOPE_DIM = 64
ROPE_DIM = 32
V_DIM = 64
D_FF = 2816
N_EXPERTS = 8
TOP_K = 2
D_FF_EXPERT = 3584

kernel_name = 'hybrid_dit_gqa_natten_mla_moe'


def _rmsnorm(x, g):
    xf = x.astype(jnp.float32)
    y = xf * lax.rsqrt(jnp.mean(xf * xf, axis=-1, keepdims=True) + NORM_EPS)
    return (y * g.astype(jnp.float32)).astype(x.dtype)


def _modulate(h, shift, scale):
    return h * (1 + scale) + shift


def _adaln(cvec, w_mod, b_mod):
    return jnp.split(jax.nn.silu(cvec) @ w_mod + b_mod, 6, axis=-1)


def _rope_1d(x, pos):
    half = x.shape[-1] // 2
    inv_freq = jnp.exp(-math.log(ROPE_THETA) * jnp.arange(half, dtype=jnp.float32) / half)
    ang = pos.astype(jnp.float32)[:, None] * inv_freq[None, :]
    cos, sin = jnp.cos(ang), jnp.sin(ang)
    xf = x.astype(jnp.float32)
    x1, x2 = xf[..., :half], xf[..., half:]
    return jnp.concatenate([x1 * cos - x2 * sin, x1 * sin + x2 * cos], axis=-1).astype(x.dtype)


def _rope_2d(x, rows, cols):
    h = x.shape[-1] // 2
    return jnp.concatenate([_rope_1d(x[..., :h], rows), _rope_1d(x[..., h:], cols)], axis=-1)


def _block_attention(q, k, v):
    B, G, R, T, dk = q.shape
    dv = v.shape[-1]
    nb = T // Q_BLOCK
    scale = dk ** -0.5
    qb = jnp.moveaxis(q.reshape(B, G, R, nb, Q_BLOCK, dk), 3, 0)

    def one_block(qi):
        s = jnp.einsum('bgrqd,bgkd->bgrqk', qi, k, preferred_element_type=jnp.float32) * scale
        p = jax.nn.softmax(s, axis=-1)
        return jnp.einsum('bgrqk,bgkd->bgrqd', p.astype(v.dtype), v)

    o = lax.map(one_block, qb)
    return jnp.moveaxis(o, 0, 3).reshape(B, G, R, T, dv)


def _gqa_project(h, w_qkv, q_gain, k_gain):
    B, N, _ = h.shape
    q, k, v = jnp.split(h @ w_qkv, [A_HEADS * HEAD_DIM, (A_HEADS + A_KV_HEADS) * HEAD_DIM], axis=-1)
    q = _rmsnorm(q.reshape(B, N, A_KV_HEADS, A_REP, HEAD_DIM), q_gain).transpose(0, 2, 3, 1, 4)
    k = _rmsnorm(k.reshape(B, N, A_KV_HEADS, HEAD_DIM), k_gain).transpose(0, 2, 1, 3)
    v = v.reshape(B, N, A_KV_HEADS, HEAD_DIM).transpose(0, 2, 1, 3)
    return q, k, v


def _gqa_mixer(h_lat, h_ctx, rows, cols, need_ctx, w_qkv, q_gain, k_gain, w_o):
    B, T, _ = h_lat.shape
    q_l, k_l, v_l = _gqa_project(h_lat, w_qkv, q_gain, k_gain)
    q_l = _rope_2d(q_l, rows, cols)
    k_l = _rope_2d(k_l, rows, cols)
    q_c, k_c, v_c = _gqa_project(h_ctx, w_qkv, q_gain, k_gain)
    k_all = jnp.concatenate([k_c, k_l], axis=2)
    v_all = jnp.concatenate([v_c, v_l], axis=2)
    o_l = _block_attention(q_l, k_all, v_all)
    o_l = o_l.transpose(0, 3, 1, 2, 4).reshape(B, T, A_HEADS * HEAD_DIM) @ w_o
    o_c = None
    if need_ctx:
        n_ctx = h_ctx.shape[1]
        o_c = _block_attention(q_c, k_c, v_c)
        o_c = o_c.transpose(0, 3, 1, 2, 4).reshape(B, n_ctx, A_HEADS * HEAD_DIM) @ w_o
    return o_l, o_c


def _mha_project(h, w_qkv, q_gain, k_gain):
    B, N, _ = h.shape
    q, k, v = jnp.split(h @ w_qkv, 3, axis=-1)
    q = _rmsnorm(q.reshape(B, N, B_HEADS, HEAD_DIM), q_gain).transpose(0, 2, 1, 3)
    k = _rmsnorm(k.reshape(B, N, B_HEADS, HEAD_DIM), k_gain).transpose(0, 2, 1, 3)
    v = v.reshape(B, N, B_HEADS, HEAD_DIM).transpose(0, 2, 1, 3)
    return q, k, v


def _natten_mixer(h_lat, h_ctx, need_ctx, w_qkv, q_gain, k_gain, rel_bias, w_o):
    B, T, _ = h_lat.shape
    rows_n = T // GRID_W
    wr = min(WIN_R, rows_n)
    scale = HEAD_DIM ** -0.5
    q_l, k_l, v_l = _mha_project(h_lat, w_qkv, q_gain, k_gain)
    q_c, k_c, v_c = _mha_project(h_ctx, w_qkv, q_gain, k_gain)
    n_ctx = k_c.shape[2]
    kg = k_l.reshape(B, B_HEADS, rows_n, GRID_W, HEAD_DIM)
    vg = v_l.reshape(B, B_HEADS, rows_n, GRID_W, HEAD_DIM)
    qg = jnp.moveaxis(q_l.reshape(B, B_HEADS, rows_n, GRID_W, HEAD_DIM), 2, 0)
    col = jnp.arange(GRID_W, dtype=jnp.int32)
    c0 = jnp.clip(col - WIN_C // 2, 0, GRID_W - WIN_C)
    col_ok = (col[None, :] >= c0[:, None]) & (col[None, :] < c0[:, None] + WIN_C)
    col_idx = jnp.clip(col[None, :] - col[:, None] + WIN_C - 1, 0, 2 * WIN_C - 2)
    band_ok = jnp.tile(col_ok, (1, wr))

    def one_row(args):
        r, qr = args
        r0 = jnp.clip(r - wr // 2, 0, rows_n - wr)
        kr = lax.dynamic_slice_in_dim(kg, r0, wr, axis=2).reshape(B, B_HEADS, wr * GRID_W, HEAD_DIM)
        vr = lax.dynamic_slice_in_dim(vg, r0, wr, axis=2).reshape(B, B_HEADS, wr * GRID_W, HEAD_DIM)
        row_idx = r0 + jnp.arange(wr, dtype=jnp.int32) - r + WIN_R - 1
        bias = rel_bias[:, row_idx][:, :, col_idx]
        bias = bias.transpose(0, 2, 1, 3).reshape(B_HEADS, GRID_W, wr * GRID_W).astype(jnp.float32)
        s_l = jnp.einsum('bhqd,bhkd->bhqk', qr, kr, preferred_element_type=jnp.float32) * scale + bias
        s_l = jnp.where(band_ok, s_l, NEG_INF)
        s_c = jnp.einsum('bhqd,bhkd->bhqk', qr, k_c, preferred_element_type=jnp.float32) * scale
        p = jax.nn.softmax(jnp.concatenate([s_c, s_l], axis=-1), axis=-1).astype(v_l.dtype)
        return (jnp.einsum('bhqk,bhkd->bhqd', p[..., :n_ctx], v_c)
                + jnp.einsum('bhqk,bhkd->bhqd', p[..., n_ctx:], vr))

    o = lax.map(one_row, (jnp.arange(rows_n, dtype=jnp.int32), qg))
    o_l = o.transpose(1, 0, 3, 2, 4).reshape(B, T, B_HEADS * HEAD_DIM) @ w_o
    o_c = None
    if need_ctx:
        o_c = _block_attention(q_c[:, :, None], k_c, v_c)[:, :, 0]
        o_c = o_c.transpose(0, 2, 1, 3).reshape(B, n_ctx, B_HEADS * HEAD_DIM) @ w_o
    return o_l, o_c


def _mla_project(h, w_in, g_dq, g_dkv, w_uq, w_ukv, q_gain, k_gain):
    B, N, _ = h.shape
    cq, ckv, kr = jnp.split(h @ w_in, [Q_LORA, Q_LORA + KV_LORA], axis=-1)
    q = (_rmsnorm(cq, g_dq) @ w_uq).reshape(B, N, C_HEADS, NOPE_DIM + ROPE_DIM)
    kv = (_rmsnorm(ckv, g_dkv) @ w_ukv).reshape(B, N, C_HEADS, NOPE_DIM + V_DIM)
    k_nope, v = jnp.split(kv, [NOPE_DIM], axis=-1)
    q_nope = _rmsnorm(q[..., :NOPE_DIM], q_gain[:NOPE_DIM]).transpose(0, 2, 1, 3)
    q_rope = _rmsnorm(q[..., NOPE_DIM:], q_gain[NOPE_DIM:]).transpose(0, 2, 1, 3)
    k_nope = _rmsnorm(k_nope, k_gain[:NOPE_DIM]).transpose(0, 2, 1, 3)
    k_rope = _rmsnorm(kr, k_gain[NOPE_DIM:])
    return q_nope, q_rope, k_nope, k_rope, v.transpose(0, 2, 1, 3)


def _mla_assemble(q_nope, q_rope, k_nope, k_rope):
    q = jnp.concatenate([q_nope, q_rope], axis=-1)[:, :, None]
    k_rope_h = jnp.broadcast_to(k_rope[:, None], k_nope.shape[:-1] + (ROPE_DIM,))
    k = jnp.concatenate([k_nope, k_rope_h], axis=-1)
    return q, k


def _mla_mixer(h_lat, h_ctx, rows, cols, need_ctx, w_in, g_dq, g_dkv, w_uq, w_ukv, q_gain, k_gain, w_o):
    B, T, _ = h_lat.shape
    qn_l, qr_l, kn_l, kr_l, v_l = _mla_project(h_lat, w_in, g_dq, g_dkv, w_uq, w_ukv, q_gain, k_gain)
    qn_c, qr_c, kn_c, kr_c, v_c = _mla_project(h_ctx, w_in, g_dq, g_dkv, w_uq, w_ukv, q_gain, k_gain)
    q_l, k_l = _mla_assemble(qn_l, _rope_2d(qr_l, rows, cols), kn_l, _rope_2d(kr_l, rows, cols))
    q_c, k_c = _mla_assemble(qn_c, qr_c, kn_c, kr_c)
    k_all = jnp.concatenate([k_c, k_l], axis=2)
    v_all = jnp.concatenate([v_c, v_l], axis=2)
    o_l = _block_attention(q_l, k_all, v_all)[:, :, 0]
    o_l = o_l.transpose(0, 2, 1, 3).reshape(B, T, C_HEADS * V_DIM) @ w_o
    o_c = None
    if need_ctx:
        n_ctx = h_ctx.shape[1]
        o_c = _block_attention(q_c, k_c, v_c)[:, :, 0]
        o_c = o_c.transpose(0, 2, 1, 3).reshape(B, n_ctx, C_HEADS * V_DIM) @ w_o
    return o_l, o_c


def _swiglu(h, w_gu, w_down):
    g, u = jnp.split(h @ w_gu, 2, axis=-1)
    return (jax.nn.silu(g) * u) @ w_down


def _moe(h, w_router, we_gu, we_down):
    shape = h.shape
    t = h.reshape(-1, shape[-1])
    logits = jnp.dot(t, w_router, preferred_element_type=jnp.float32)
    top_val, top_idx = lax.top_k(logits, TOP_K)
    top_w = jax.nn.softmax(top_val, axis=-1)
    combine = jnp.einsum('nk,nke->ne', top_w,
                         jax.nn.one_hot(top_idx, N_EXPERTS, dtype=jnp.float32)).astype(h.dtype)
    out = jnp.zeros_like(t)
    for e in range(N_EXPERTS):
        out = out + combine[:, e:e + 1] * _swiglu(t, we_gu[e], we_down[e])
    return out.reshape(shape)


def setup_inputs(seed: int = 0) -> dict:
    key = jax.random.key(seed)
    keys = jax.random.split(key, 96)
    counter = iter(range(96))

    def rnd(shape, scale):
        return jax.random.normal(keys[next(counter)], shape, jnp.float32) * scale

    def gain(n):
        return 1.0 + rnd((n,), 0.05)

    D = D_MODEL
    inp = {}
    inp['x'] = rnd((BATCH, SEQ, D), 1.0)
    inp['c'] = rnd((BATCH, D), 1.0)
    inp['ctx'] = rnd((BATCH, CTX_LEN, D), 1.0)
    inp['c_ctx'] = rnd((D,), 1.0)
    for i in range(DEPTH):
        p = 'l%d_' % i
        inp[p + 'w_mod'] = rnd((D, 6 * D), 0.5 * D ** -0.5)
        inp[p + 'b_mod'] = rnd((6 * D,), 0.02)
        inp[p + 'g_mix'] = gain(D)
        inp[p + 'g_ffn'] = gain(D)
        kind = i % N_MIXERS
        if kind == 0:
            inp[p + 'w_qkv'] = rnd((D, (A_HEADS + 2 * A_KV_HEADS) * HEAD_DIM), D ** -0.5)
            inp[p + 'q_gain'] = gain(HEAD_DIM)
            inp[p + 'k_gain'] = gain(HEAD_DIM)
            inp[p + 'w_o'] = rnd((A_HEADS * HEAD_DIM, D), (A_HEADS * HEAD_DIM) ** -0.5)
        elif kind == 1:
            inp[p + 'w_qkv'] = rnd((D, 3 * B_HEADS * HEAD_DIM), D ** -0.5)
            inp[p + 'q_gain'] = gain(HEAD_DIM)
            inp[p + 'k_gain'] = gain(HEAD_DIM)
            inp[p + 'rel_bias'] = rnd((B_HEADS, 2 * WIN_R - 1, 2 * WIN_C - 1), 0.2)
            inp[p + 'w_o'] = rnd((B_HEADS * HEAD_DIM, D), (B_HEADS * HEAD_DIM) ** -0.5)
        else:
            inp[p + 'w_in'] = rnd((D, Q_LORA + KV_LORA + ROPE_DIM), D ** -0.5)
            inp[p + 'g_dq'] = gain(Q_LORA)
            inp[p + 'g_dkv'] = gain(KV_LORA)
            inp[p + 'w_uq'] = rnd((Q_LORA, C_HEADS * (NOPE_DIM + ROPE_DIM)), Q_LORA ** -0.5)
            inp[p + 'w_ukv'] = rnd((KV_LORA, C_HEADS * (NOPE_DIM + V_DIM)), KV_LORA ** -0.5)
            inp[p + 'q_gain'] = gain(NOPE_DIM + ROPE_DIM)
            inp[p + 'k_gain'] = gain(NOPE_DIM + ROPE_DIM)
            inp[p + 'w_o'] = rnd((C_HEADS * V_DIM, D), (C_HEADS * V_DIM) ** -0.5)
        if i % 2 == 0:
            inp[p + 'w_gu'] = rnd((D, 2 * D_FF), D ** -0.5)
            inp[p + 'w_down'] = rnd((D_FF, D), D_FF ** -0.5)
        else:
            inp[p + 'w_router'] = rnd((D, N_EXPERTS), D ** -0.5)
            inp[p + 'we_gu'] = rnd((N_EXPERTS, D, 2 * D_FF_EXPERT), D ** -0.5)
            inp[p + 'we_down'] = rnd((N_EXPERTS, D_FF_EXPERT, D), D_FF_EXPERT ** -0.5)
    return inp


def reference(x, c, ctx, c_ctx,
              l0_w_mod, l0_b_mod, l0_g_mix, l0_g_ffn, l0_w_qkv, l0_q_gain, l0_k_gain, l0_w_o, l0_w_gu, l0_w_down,
              l1_w_mod, l1_b_mod, l1_g_mix, l1_g_ffn, l1_w_qkv, l1_q_gain, l1_k_gain, l1_rel_bias, l1_w_o,
              l1_w_router, l1_we_gu, l1_we_down,
              l2_w_mod, l2_b_mod, l2_g_mix, l2_g_ffn, l2_w_in, l2_g_dq, l2_g_dkv, l2_w_uq, l2_w_ukv, l2_q_gain,
              l2_k_gain, l2_w_o, l2_w_gu, l2_w_down,
              l3_w_mod, l3_b_mod, l3_g_mix, l3_g_ffn, l3_w_qkv, l3_q_gain, l3_k_gain, l3_w_o,
              l3_w_router, l3_we_gu, l3_we_down):
    layers = [
        dict(w_mod=l0_w_mod, b_mod=l0_b_mod, g_mix=l0_g_mix, g_ffn=l0_g_ffn,
             mix=dict(w_qkv=l0_w_qkv, q_gain=l0_q_gain, k_gain=l0_k_gain, w_o=l0_w_o),
             ffn=dict(w_gu=l0_w_gu, w_down=l0_w_down)),
        dict(w_mod=l1_w_mod, b_mod=l1_b_mod, g_mix=l1_g_mix, g_ffn=l1_g_ffn,
             mix=dict(w_qkv=l1_w_qkv, q_gain=l1_q_gain, k_gain=l1_k_gain, rel_bias=l1_rel_bias, w_o=l1_w_o),
             ffn=dict(w_router=l1_w_router, we_gu=l1_we_gu, we_down=l1_we_down)),
        dict(w_mod=l2_w_mod, b_mod=l2_b_mod, g_mix=l2_g_mix, g_ffn=l2_g_ffn,
             mix=dict(w_in=l2_w_in, g_dq=l2_g_dq, g_dkv=l2_g_dkv, w_uq=l2_w_uq, w_ukv=l2_w_ukv,
                      q_gain=l2_q_gain, k_gain=l2_k_gain, w_o=l2_w_o),
             ffn=dict(w_gu=l2_w_gu, w_down=l2_w_down)),
        dict(w_mod=l3_w_mod, b_mod=l3_b_mod, g_mix=l3_g_mix, g_ffn=l3_g_ffn,
             mix=dict(w_qkv=l3_w_qkv, q_gain=l3_q_gain, k_gain=l3_k_gain, w_o=l3_w_o),
             ffn=dict(w_router=l3_w_router, we_gu=l3_we_gu, we_down=l3_we_down)),
    ]
    T = x.shape[1]
    pos = jnp.arange(T, dtype=jnp.int32)
    rows = pos // GRID_W
    cols = pos % GRID_W
    x_lat, x_ctx = x, ctx
    for i in range(DEPTH):
        p = layers[i]
        need_ctx = i < DEPTH - 1
        sh1, sc1, g1, sh2, sc2, g2 = [m[:, None, :] for m in _adaln(c, p['w_mod'], p['b_mod'])]
        csh1, csc1, cg1, csh2, csc2, cg2 = _adaln(c_ctx, p['w_mod'], p['b_mod'])
        h_lat = _modulate(_rmsnorm(x_lat, p['g_mix']), sh1, sc1)
        h_ctx = _modulate(_rmsnorm(x_ctx, p['g_mix']), csh1, csc1)
        kind = i % N_MIXERS
        if kind == 0:
            o_lat, o_ctx = _gqa_mixer(h_lat, h_ctx, rows, cols, need_ctx, **p['mix'])
        elif kind == 1:
            o_lat, o_ctx = _natten_mixer(h_lat, h_ctx, need_ctx, **p['mix'])
        else:
            o_lat, o_ctx = _mla_mixer(h_lat, h_ctx, rows, cols, need_ctx, **p['mix'])
        x_lat = x_lat + g1 * o_lat
        channel = _swiglu if i % 2 == 0 else _moe
        h2_lat = _modulate(_rmsnorm(x_lat, p['g_ffn']), sh2, sc2)
        if need_ctx:
            x_ctx = x_ctx + cg1 * o_ctx
            h2_ctx = _modulate(_rmsnorm(x_ctx, p['g_ffn']), csh2, csc2)
            n_ctx = x_ctx.shape[1]
            f = channel(jnp.concatenate([h2_ctx, h2_lat], axis=1), **p['ffn'])
            x_ctx = x_ctx + cg2 * f[:, :n_ctx]
            x_lat = x_lat + g2 * f[:, n_ctx:]
        else:
            x_lat = x_lat + g2 * channel(h2_lat, **p['ffn'])
    return x_lat
```

```python
import functools
import math

import jax
import jax.numpy as jnp
from jax import lax
from jax.experimental import pallas as pl
from jax.experimental.pallas import tpu as pltpu

F32 = jnp.float32
BF16 = jnp.bfloat16

LANES = 128
SUBLANES = 8
VMEM_LIMIT_BYTES = 56 * 1024 * 1024

NORM_EPS = 1e-6
ROPE_THETA = 10000.0
NEG_INF = -1e30
GRID_W = 64
HEAD_DIM = 64
WIN_R = 8
WIN_C = 16
A_HEADS = 16
A_KV_HEADS = 4
B_HEADS = 16
C_HEADS = 16
Q_LORA = 384
KV_LORA = 256
NOPE_DIM = 64
ROPE_DIM = 32
V_DIM = 64
N_EXPERTS = 8
N_PAIRS = 8

SH1, SC1, G1, SH2, SC2, G2 = range(6)


def _params(sem):
    return pltpu.CompilerParams(dimension_semantics=sem, vmem_limit_bytes=VMEM_LIMIT_BYTES)


def _dot(a, b):
    return jnp.dot(a, b, preferred_element_type=F32)


def _dot_nt(a, b):
    return lax.dot_general(a, b, (((1,), (1,)), ((), ())), preferred_element_type=F32)


def _silu(x):
    return x * (1.0 / (1.0 + jnp.exp(-x)))


def _norm_mod(x, g, shift, scale):
    ms = jnp.mean(x * x, axis=-1, keepdims=True)
    y = x * lax.rsqrt(ms + NORM_EPS) * g
    return y * (1.0 + scale) + shift


def _seg_mean_sq(y, seg):
    sq = y * y
    hi = sq.astype(BF16)
    lo = (sq - hi.astype(F32)).astype(BF16)
    return _dot(hi, seg) + _dot(lo, seg)


def _rope(y, cos, sin_signed, shift):
    lane = lax.broadcasted_iota(jnp.int32, y.shape, 1)
    first = (lane % (2 * shift)) < shift
    partner = jnp.where(first, pltpu.roll(y, LANES - shift, 1), pltpu.roll(y, shift, 1))
    return y * cos + partner * sin_signed


class _Layout:
    def __init__(self, batch, seq, ctx_len, d_model):
        self.B, self.T, self.TC, self.D = batch, seq, ctx_len, d_model
        self.NL = batch * seq
        self.NC = batch * ctx_len
        self.N = self.NL + self.NC
        self.tm = 1024 if (self.NC % 1024 == 0 and seq % 1024 == 0) else ctx_len
        assert seq % self.tm == 0 and self.NC % self.tm == 0
        self.tpb = seq // self.tm
        self.tq = ctx_len
        assert seq % self.tq == 0 and seq % GRID_W == 0

    def mod_group(self, i):
        return jnp.minimum(i // self.tpb, self.B)

    def rope_block(self, i):
        return jnp.where(i < self.B * self.tpb, i % self.tpb, self.tpb)

    def rows(self, with_ctx):
        return self.N if with_ctx else self.NL

    def tiles(self, with_ctx):
        return self.rows(with_ctx) // self.tm


def _adaln_kernel(c_ref, w_ref, b_ref, o_ref):
    s = _silu(c_ref[...])
    o_ref[...] = jnp.dot(s, w_ref[...], preferred_element_type=F32,
                         precision=lax.Precision.HIGHEST) + b_ref[...]


def _adaln(cvec, w_mod, b_mod, n_groups):
    d, n_out = w_mod.shape
    tn = n_out // 4
    out = pl.pallas_call(
        _adaln_kernel,
        out_shape=jax.ShapeDtypeStruct((SUBLANES, n_out), F32),
        grid=(n_out // tn,),
        in_specs=[pl.BlockSpec((SUBLANES, d), lambda j: (0, 0)),
                  pl.BlockSpec((d, tn), lambda j: (0, j)),
                  pl.BlockSpec((1, tn), lambda j: (0, j))],
        out_specs=pl.BlockSpec((SUBLANES, tn), lambda j: (0, j)),
        compiler_params=_params(("arbitrary",)),
        name="adaln",
    )(cvec, w_mod, b_mod.reshape(1, n_out))
    mod = out[:n_groups].reshape(n_groups, 6, d)
    return jnp.pad(mod, ((0, 0), (0, 2), (0, 0)))


def _proj_kernel(*refs, has_mod, n_norm, n_col_tiles, rope_shift, has_add, groups):
    refs = list(refs)
    x_ref = refs.pop(0)
    if has_mod:
        mod_ref, g_ref = refs.pop(0), refs.pop(0)
    w_ref = refs.pop(0)
    if n_norm:
        seg_ref, gain_ref = refs.pop(0), refs.pop(0)
    if rope_shift:
        cos_ref, sin_ref = refs.pop(0), refs.pop(0)
    if has_add:
        add_ref = refs.pop(0)
    o_ref = refs.pop(0)
    j = pl.program_id(1)

    if has_mod:
        h_scr = refs.pop(0)

        @pl.when(j == 0)
        def _():
            h = _norm_mod(x_ref[...], g_ref[...], mod_ref[SH1:SH1 + 1, :], mod_ref[SC1:SC1 + 1, :])
            h_scr[...] = h.astype(BF16)

        h = h_scr[...]
    else:
        h = x_ref[...]
    acc = _dot(h, w_ref[...].astype(BF16))

    def normed():
        for c in range(groups):
            sl = slice(c * LANES, (c + 1) * LANES)
            y = acc[:, sl]
            y = y * lax.rsqrt(_seg_mean_sq(y, seg_ref[...]) + NORM_EPS) * gain_ref[:, sl]
            if rope_shift:
                y = _rope(y, cos_ref[...], sin_ref[...], rope_shift)
            if has_add:
                y = y + add_ref[...]
            o_ref[:, sl] = y.astype(o_ref.dtype)

    def plain():
        o_ref[...] = acc.astype(o_ref.dtype)

    if n_norm == 0:
        plain()
    elif n_norm == n_col_tiles:
        normed()
    else:
        pl.when(j < n_norm)(normed)
        pl.when(j >= n_norm)(plain)


def _proj(L, name, x, w, *, with_ctx=True, tn=512, mod=None, g=None, seg=None, gain=None, n_norm=0,
          rope=None, add=None, out_dtype=BF16):
    k, n_out = w.shape
    tm = L.tm
    n_col = n_out // tn
    in_specs = [pl.BlockSpec((tm, k), lambda i, j: (i, 0))]
    args = [x]
    scratch = []
    if mod is not None:
        in_specs += [pl.BlockSpec((None, SUBLANES, k), lambda i, j: (L.mod_group(i), 0, 0)),
                     pl.BlockSpec((1, k), lambda i, j: (0, 0))]
        args += [mod, g.reshape(1, k)]
        scratch.append(pltpu.VMEM((tm, k), BF16))
    in_specs.append(pl.BlockSpec((k, tn), lambda i, j: (0, j)))
    args.append(w)
    if n_norm:
        in_specs += [pl.BlockSpec((LANES, LANES), lambda i, j: (0, 0)),
                     pl.BlockSpec((1, tn), lambda i, j: (0, j))]
        args += [seg, gain]
    rope_shift = 0
    if rope is not None:
        cos, sin, rope_shift = rope
        in_specs += [pl.BlockSpec((tm, LANES), lambda i, j: (L.rope_block(i), 0))] * 2
        args += [cos, sin]
    if add is not None:
        in_specs.append(pl.BlockSpec((tm, LANES), lambda i, j: (i, 0)))
        args.append(add)
    kern = functools.partial(_proj_kernel, has_mod=mod is not None, n_norm=n_norm, n_col_tiles=n_col,
                             rope_shift=rope_shift, has_add=add is not None, groups=tn // LANES)
    return pl.pallas_call(
        kern,
        out_shape=jax.ShapeDtypeStruct((L.N, n_out), out_dtype),
        grid=(L.tiles(with_ctx), n_col),
        in_specs=in_specs,
        out_specs=pl.BlockSpec((tm, tn), lambda i, j: (i, j)),
        scratch_shapes=scratch,
        compiler_params=_params(("parallel", "arbitrary")),
        name=name,
    )(*args)


def _mla_in_kernel(x_ref, mod_ref, g_ref, w_ref, gdq_ref, gdkv_ref, seg_ref, gain_ref, cos_ref, sin_ref,
                   cq_ref, ckv_ref, kr_ref):
    h = _norm_mod(x_ref[...], g_ref[...], mod_ref[SH1:SH1 + 1, :], mod_ref[SC1:SC1 + 1, :])
    acc = _dot(h.astype(BF16), w_ref[...].astype(BF16))
    cq = acc[:, :Q_LORA]
    cq = cq * lax.rsqrt(jnp.mean(cq * cq, axis=-1, keepdims=True) + NORM_EPS) * gdq_ref[...]
    cq_ref[...] = cq.astype(BF16)
    ckv = acc[:, Q_LORA:Q_LORA + KV_LORA]
    ckv = ckv * lax.rsqrt(jnp.mean(ckv * ckv, axis=-1, keepdims=True) + NORM_EPS) * gdkv_ref[...]
    ckv_ref[...] = ckv.astype(BF16)
    kr = acc[:, Q_LORA + KV_LORA:]
    kr = kr * lax.rsqrt(_seg_mean_sq(kr, seg_ref[...]) + NORM_EPS) * gain_ref[...]
    kr_ref[...] = _rope(kr, cos_ref[...], sin_ref[...], ROPE_DIM // 4)


def _mla_in(L, x, mod, g, w_in_p, g_dq, g_dkv, seg_kr, gain_kr, cos, sin):
    tm, d = L.tm, L.D
    n_out = w_in_p.shape[1]
    full = lambda shape: pl.BlockSpec(shape, lambda i: (0,) * len(shape))
    return pl.pallas_call(
        _mla_in_kernel,
        out_shape=(jax.ShapeDtypeStruct((L.N, Q_LORA), BF16),
                   jax.ShapeDtypeStruct((L.N, KV_LORA), BF16),
                   jax.ShapeDtypeStruct((L.N, LANES), F32)),
        grid=(L.tiles(True),),
        in_specs=[pl.BlockSpec((tm, d), lambda i: (i, 0)),
                  pl.BlockSpec((None, SUBLANES, d), lambda i: (L.mod_group(i), 0, 0)),
                  full((1, d)), full((d, n_out)), full((1, Q_LORA)), full((1, KV_LORA)),
                  full((LANES, LANES)), full((1, LANES)),
                  pl.BlockSpec((tm, LANES), lambda i: (L.rope_block(i), 0)),
                  pl.BlockSpec((tm, LANES), lambda i: (L.rope_block(i), 0))],
        out_specs=(pl.BlockSpec((tm, Q_LORA), lambda i: (i, 0)),
                   pl.BlockSpec((tm, KV_LORA), lambda i: (i, 0)),
                   pl.BlockSpec((tm, LANES), lambda i: (i, 0))),
        compiler_params=_params(("parallel",)),
        name="mla_in",
    )(x, mod, g.reshape(1, d), w_in_p, g_dq.reshape(1, -1), g_dkv.reshape(1, -1), seg_kr, gain_kr, cos, sin)


def _attn_kernel(*refs, nq, nk, pair, lat_keys, nq_lat, ctx_q):
    refs = list(refs)
    q_refs = [refs.pop(0) for _ in range(nq)]
    kl_refs = [refs.pop(0) for _ in range(nk)] if lat_keys else []
    kc_refs = [refs.pop(0) for _ in range(nk)]
    vl_ref = refs.pop(0) if lat_keys else None
    vc_ref = refs.pop(0)
    o_ref = refs.pop(0)
    lo = lax.broadcasted_iota(jnp.int32, (1, LANES), 1) < HEAD_DIM

    def run(use_lat):
        acc = None
        for hh in range(2):
            half = lo if hh == 0 else jnp.logical_not(lo)
            q = q_refs[hh % nq][...]
            if pair:
                q = jnp.where(half, q, jnp.zeros_like(q))
            s_c = _dot_nt(q, kc_refs[hh % nk][...])
            m = jnp.max(s_c, axis=-1, keepdims=True)
            if use_lat:
                s_l = _dot_nt(q, kl_refs[hh % nk][...])
                m = jnp.maximum(m, jnp.max(s_l, axis=-1, keepdims=True))
            p_c = jnp.exp(s_c - m)
            den = jnp.sum(p_c, axis=-1, keepdims=True)
            vc = vc_ref[...]
            o = _dot(p_c.astype(BF16), jnp.where(half, vc, jnp.zeros_like(vc)))
            if use_lat:
                p_l = jnp.exp(s_l - m)
                den = den + jnp.sum(p_l, axis=-1, keepdims=True)
                vl = vl_ref[...]
                o = o + _dot(p_l.astype(BF16), jnp.where(half, vl, jnp.zeros_like(vl)))
            o = o / den
            acc = o if acc is None else acc + o
        o_ref[...] = acc.astype(o_ref.dtype)

    if lat_keys and ctx_q:
        i = pl.program_id(2)
        pl.when(i < nq_lat)(lambda: run(True))
        pl.when(i == nq_lat)(lambda: run(False))
    else:
        run(lat_keys)


def _attention(L, name, q, q_cols, k, k_cols, v, v_col, *, pair, mode):
    B, T, TC, tq = L.B, L.T, L.TC, L.tq
    nq_lat = T // tq
    ctx_row = L.NL // TC
    lat_keys = mode != "ctx"
    ctx_q = mode != "lat"
    n_i = {"all": nq_lat + 1, "lat": nq_lat, "ctx": 1}[mode]

    def q_row(b, i):
        if mode == "ctx":
            return ctx_row + b
        if mode == "lat":
            return b * nq_lat + i
        return jnp.where(i < nq_lat, b * nq_lat + i, ctx_row + b)

    in_specs, args = [], []
    for f in q_cols:
        in_specs.append(pl.BlockSpec((tq, LANES), lambda b, p, i, f=f: (q_row(b, i), f(p))))
        args.append(q)
    if lat_keys:
        for f in k_cols:
            in_specs.append(pl.BlockSpec((T, LANES), lambda b, p, i, f=f: (b, f(p))))
            args.append(k)
    for f in k_cols:
        in_specs.append(pl.BlockSpec((TC, LANES), lambda b, p, i, f=f: (ctx_row + b, f(p))))
        args.append(k)
    if lat_keys:
        in_specs.append(pl.BlockSpec((T, LANES), lambda b, p, i: (b, v_col(p))))
        args.append(v)
    in_specs.append(pl.BlockSpec((TC, LANES), lambda b, p, i: (ctx_row + b, v_col(p))))
    args.append(v)
    out_rows = {"all": L.N, "lat": L.NL, "ctx": L.NC}[mode]
    out_row = (lambda b, i: b) if mode == "ctx" else q_row
    kern = functools.partial(_attn_kernel, nq=len(q_cols), nk=len(k_cols), pair=pair, lat_keys=lat_keys,
                             nq_lat=nq_lat, ctx_q=ctx_q)
    return pl.pallas_call(
        kern,
        out_shape=jax.ShapeDtypeStruct((out_rows, N_PAIRS * LANES), BF16),
        grid=(B, N_PAIRS, n_i),
        in_specs=in_specs,
        out_specs=pl.BlockSpec((tq, LANES), lambda b, p, i: (out_row(b, i), p)),
        compiler_params=_params(("parallel", "parallel", "arbitrary")),
        name=name,
    )(*args)


def _natten_kernel(q_ref, k_ref, v_ref, kc_ref, vc_ref, bias_ref, o_ref, *, rows_n):
    lo = lax.broadcasted_iota(jnp.int32, (1, LANES), 1) < HEAD_DIM
    halves = (lo, jnp.logical_not(lo))
    kc = kc_ref[...]
    vc = vc_ref[...]
    vc_h = [jnp.where(h, vc, jnp.zeros_like(vc)) for h in halves]
    win = WIN_R * GRID_W

    def body(r, carry):
        r0 = jnp.clip(r - WIN_R // 2, 0, rows_n - WIN_R)
        q = q_ref[pl.ds(pl.multiple_of(r * GRID_W, GRID_W), GRID_W), :]
        kw = k_ref[pl.ds(pl.multiple_of(r0 * GRID_W, GRID_W), win), :]
        vw = v_ref[pl.ds(pl.multiple_of(r0 * GRID_W, GRID_W), win), :]
        acc = None
        for hh in range(2):
            qh = jnp.where(halves[hh], q, jnp.zeros_like(q))
            s_l = _dot_nt(qh, kw) + bias_ref[hh, r - r0]
            s_c = _dot_nt(qh, kc)
            m = jnp.maximum(jnp.max(s_l, axis=-1, keepdims=True), jnp.max(s_c, axis=-1, keepdims=True))
            p_l = jnp.exp(s_l - m)
            p_c = jnp.exp(s_c - m)
            den = jnp.sum(p_l, axis=-1, keepdims=True) + jnp.sum(p_c, axis=-1, keepdims=True)
            o = _dot(p_l.astype(BF16), jnp.where(halves[hh], vw, jnp.zeros_like(vw)))
            o = (o + _dot(p_c.astype(BF16), vc_h[hh])) / den
            acc = o if acc is None else acc + o
        o_ref[pl.ds(pl.multiple_of(r * GRID_W, GRID_W), GRID_W), :] = acc.astype(o_ref.dtype)
        return carry

    lax.fori_loop(0, rows_n, body, 0)


def _natten(L, qkv, bias):
    B, T, TC = L.B, L.T, L.TC
    ctx_row = L.NL // TC
    win = WIN_R * GRID_W
    return pl.pallas_call(
        functools.partial(_natten_kernel, rows_n=T // GRID_W),
        out_shape=jax.ShapeDtypeStruct((L.NL, N_PAIRS * LANES), BF16),
        grid=(B, N_PAIRS),
        in_specs=[pl.BlockSpec((T, LANES), lambda b, p: (b, p)),
                  pl.BlockSpec((T, LANES), lambda b, p: (b, N_PAIRS + p)),
                  pl.BlockSpec((T, LANES), lambda b, p: (b, 2 * N_PAIRS + p)),
                  pl.BlockSpec((TC, LANES), lambda b, p: (ctx_row + b, N_PAIRS + p)),
                  pl.BlockSpec((TC, LANES), lambda b, p: (ctx_row + b, 2 * N_PAIRS + p)),
                  pl.BlockSpec((2, WIN_R, GRID_W, win), lambda b, p: (p, 0, 0, 0))],
        out_specs=pl.BlockSpec((T, LANES), lambda b, p: (b, p)),
        compiler_params=_params(("parallel", "parallel")),
        name="natten",
    )(qkv, qkv, qkv, qkv, qkv, bias)


def _oproj_kernel(o_ref, w_ref, x_ref, mod_ref, out_ref):
    acc = _dot(o_ref[...], w_ref[...].astype(BF16))
    out_ref[...] = x_ref[...] + mod_ref[G1:G1 + 1, :] * acc


def _oproj(L, o, w_o, x, mod, with_ctx):
    tm, d = L.tm // 2, L.D
    k = w_o.shape[0]
    ratio = L.tm // tm
    return pl.pallas_call(
        _oproj_kernel,
        out_shape=jax.ShapeDtypeStruct((L.rows(with_ctx), d), F32),
        grid=(L.tiles(with_ctx) * ratio,),
        in_specs=[pl.BlockSpec((tm, k), lambda i: (i, 0)),
                  pl.BlockSpec((k, d), lambda i: (0, 0)),
                  pl.BlockSpec((tm, d), lambda i: (i, 0)),
                  pl.BlockSpec((None, SUBLANES, d), lambda i: (L.mod_group(i // ratio), 0, 0))],
        out_specs=pl.BlockSpec((tm, d), lambda i: (i, 0)),
        compiler_params=_params(("parallel",)),
        name="oproj",
    )(o, w_o, x, mod)


def _ffn_kernel(x_ref, mod_ref, g_ref, wg_ref, wu_ref, wd_ref, out_ref, h_scr, acc_scr):
    c = pl.program_id(1)

    @pl.when(c == 0)
    def _():
        h = _norm_mod(x_ref[...], g_ref[...], mod_ref[SH2:SH2 + 1, :], mod_ref[SC2:SC2 + 1, :])
        h_scr[...] = h.astype(BF16)
        acc_scr[...] = jnp.zeros_like(acc_scr)

    h = h_scr[...]
    gate = _dot(h, wg_ref[...].astype(BF16))
    up = _dot(h, wu_ref[...].astype(BF16))
    act = (_silu(gate) * up).astype(BF16)
    acc_scr[...] += _dot(act, wd_ref[...].astype(BF16))

    @pl.when(c == pl.num_programs(1) - 1)
    def _():
        out_ref[...] = x_ref[...] + mod_ref[G2:G2 + 1, :] * acc_scr[...]


def _ffn(L, x, mod, g, w_gu, w_down, with_ctx):
    tm, d = L.tm, L.D
    d_ff = w_down.shape[0]
    tf = 256
    n_ch = d_ff // tf
    return pl.pallas_call(
        _ffn_kernel,
        out_shape=jax.ShapeDtypeStruct((L.rows(with_ctx), d), F32),
        grid=(L.tiles(with_ctx), n_ch),
        in_specs=[pl.BlockSpec((tm, d), lambda i, c: (i, 0)),
                  pl.BlockSpec((None, SUBLANES, d), lambda i, c: (L.mod_group(i), 0, 0)),
                  pl.BlockSpec((1, d), lambda i, c: (0, 0)),
                  pl.BlockSpec((d, tf), lambda i, c: (0, c)),
                  pl.BlockSpec((d, tf), lambda i, c: (0, n_ch + c)),
                  pl.BlockSpec((tf, d), lambda i, c: (c, 0))],
        out_specs=pl.BlockSpec((tm, d), lambda i, c: (i, 0)),
        scratch_shapes=[pltpu.VMEM((tm, d), BF16), pltpu.VMEM((tm, d), F32)],
        compiler_params=_params(("parallel", "arbitrary")),
        name="ffn",
    )(x, mod, g.reshape(1, d), w_gu, w_gu, w_down)


R_I1, R_I2, R_W1, R_W2, R_R1, R_R2 = range(6)


def _router_kernel(x_ref, mod_ref, g_ref, wr_ref, h_ref, route_ref, cnt_ref, carry_scr, *, tm):
    i = pl.program_id(0)

    @pl.when(i == 0)
    def _():
        carry_scr[...] = jnp.zeros_like(carry_scr)

    h = _norm_mod(x_ref[...], g_ref[...], mod_ref[SH2:SH2 + 1, :], mod_ref[SC2:SC2 + 1, :])
    for c in range(h.shape[1] // LANES):
        h_ref[pl.ds(c, tm, stride=SUBLANES), :] = h[:, c * LANES:(c + 1) * LANES]
    logits = jnp.dot(h, wr_ref[...], preferred_element_type=F32, precision=lax.Precision.HIGHEST)
    lane = lax.broadcasted_iota(jnp.int32, logits.shape, 1).astype(F32)
    lg = jnp.where(lane < N_EXPERTS, logits, -jnp.inf)
    m1 = jnp.max(lg, axis=-1, keepdims=True)
    i1 = jnp.min(jnp.where(lg == m1, lane, float(LANES)), axis=-1, keepdims=True)
    lg2 = jnp.where(lane == i1, -jnp.inf, lg)
    m2 = jnp.max(lg2, axis=-1, keepdims=True)
    i2 = jnp.min(jnp.where(lg2 == m2, lane, float(LANES)), axis=-1, keepdims=True)
    e = jnp.exp(m2 - m1)
    w1 = 1.0 / (1.0 + e)
    w2 = e / (1.0 + e)
    onehot = jnp.where((lane == i1) | (lane == i2), 1.0, 0.0)
    row = lax.broadcasted_iota(jnp.int32, (tm, tm), 0)
    col = lax.broadcasted_iota(jnp.int32, (tm, tm), 1)
    tri = jnp.where(row > col, 1.0, 0.0).astype(BF16)
    rank = _dot(tri, onehot.astype(BF16)) + carry_scr[0:1, :]
    r1 = jnp.sum(jnp.where(lane == i1, rank, 0.0), axis=-1, keepdims=True)
    r2 = jnp.sum(jnp.where(lane == i2, rank, 0.0), axis=-1, keepdims=True)
    carry_scr[...] = carry_scr[...] + jnp.sum(onehot, axis=0, keepdims=True)
    rec = jnp.zeros_like(logits)
    for pos, val in ((R_I1, i1), (R_I2, i2), (R_W1, w1), (R_W2, w2), (R_R1, r1), (R_R2, r2)):
        rec = jnp.where(lane == float(pos), val, rec)
    route_ref[...] = rec
    cnt_ref[...] = carry_scr[...]


def _router(L, x, mod, g, w_router, with_ctx):
    tm, d = L.tm, L.D
    wr = jnp.pad(w_router, ((0, 0), (0, LANES - w_router.shape[1])))
    return pl.pallas_call(
        functools.partial(_router_kernel, tm=tm),
        out_shape=(jax.ShapeDtypeStruct((L.rows(with_ctx) * SUBLANES, LANES), F32),
                   jax.ShapeDtypeStruct((L.rows(with_ctx), LANES), F32),
                   jax.ShapeDtypeStruct((SUBLANES, LANES), F32)),
        grid=(L.tiles(with_ctx),),
        in_specs=[pl.BlockSpec((tm, d), lambda i: (i, 0)),
                  pl.BlockSpec((None, SUBLANES, d), lambda i: (L.mod_group(i), 0, 0)),
                  pl.BlockSpec((1, d), lambda i: (0, 0)),
                  pl.BlockSpec((d, LANES), lambda i: (0, 0))],
        out_specs=(pl.BlockSpec((tm * SUBLANES, LANES), lambda i: (i, 0)),
                   pl.BlockSpec((tm, LANES), lambda i: (i, 0)),
                   pl.BlockSpec((SUBLANES, LANES), lambda i: (0, 0))),
        scratch_shapes=[pltpu.VMEM((SUBLANES, LANES), F32)],
        compiler_params=_params(("arbitrary",)),
        name="router",
    )(x, mod, g.reshape(1, d), wr)


def _row_block(ref, row):
    return ref.at[pl.ds(pl.multiple_of(row * SUBLANES, SUBLANES), SUBLANES)]


def _dispatch_kernel(pos_ref, h_ref, xs_in_ref, xs_ref, sem, *, tmd):
    del xs_in_ref
    i = pl.program_id(0)

    def copies(t):
        src = _row_block(h_ref, i * tmd + t)
        return (pltpu.make_async_copy(src, _row_block(xs_ref, pos_ref[0, 2 * t]), sem),
                pltpu.make_async_copy(src, _row_block(xs_ref, pos_ref[0, 2 * t + 1]), sem))

    def start(t, carry):
        for cp in copies(t):
            cp.start()
        return carry

    def wait(t, carry):
        for cp in copies(t):
            cp.wait()
        return carry

    lax.fori_loop(0, tmd, start, 0)
    lax.fori_loop(0, tmd, wait, 0)


def _dispatch(L, h_rows, pos, n_slots, with_ctx):
    tmd = L.tm
    n_t = L.tiles(with_ctx)
    xs0 = jnp.zeros((n_slots * SUBLANES, LANES), F32)
    return pl.pallas_call(
        functools.partial(_dispatch_kernel, tmd=tmd),
        out_shape=jax.ShapeDtypeStruct(xs0.shape, F32),
        grid=(n_t,),
        in_specs=[pl.BlockSpec((None, 1, 2 * tmd), lambda i: (i, 0, 0), memory_space=pltpu.SMEM),
                  pl.BlockSpec(memory_space=pl.ANY),
                  pl.BlockSpec(memory_space=pl.ANY)],
        out_specs=pl.BlockSpec(memory_space=pl.ANY),
        scratch_shapes=[pltpu.SemaphoreType.DMA(())],
        input_output_aliases={2: 0},
        compiler_params=pltpu.CompilerParams(dimension_semantics=("arbitrary",), has_side_effects=True),
        name="moe_dispatch",
    )(pos.reshape(-1, 1, 2 * tmd)[:n_t], h_rows, xs0)


def _experts_kernel(te_ref, tv_ref, xs_ref, wg_ref, wu_ref, wd_ref, ys_ref, h_scr, acc_scr, *, tmx):
    del te_ref
    t, c = pl.program_id(0), pl.program_id(1)
    valid = tv_ref[t] > 0
    groups = h_scr.shape[1] // LANES

    @pl.when(jnp.logical_and(valid, c == 0))
    def _():
        for k in range(groups):
            h_scr[:, k * LANES:(k + 1) * LANES] = xs_ref[pl.ds(k, tmx, stride=SUBLANES), :].astype(BF16)
        acc_scr[...] = jnp.zeros_like(acc_scr)

    @pl.when(valid)
    def _():
        h = h_scr[...]
        gate = _dot(h, wg_ref[...].astype(BF16))
        up = _dot(h, wu_ref[...].astype(BF16))
        act = (_silu(gate) * up).astype(BF16)
        acc_scr[...] += _dot(act, wd_ref[...].astype(BF16))

    @pl.when(jnp.logical_and(valid, c == pl.num_programs(1) - 1))
    def _():
        for k in range(groups):
            ys_ref[pl.ds(k, tmx, stride=SUBLANES), :] = acc_scr[:, k * LANES:(k + 1) * LANES]

    @pl.when(jnp.logical_and(jnp.logical_not(valid), c == 0))
    def _():
        ys_ref[...] = jnp.zeros_like(ys_ref)


def _experts(L, xs, tile_expert, tile_valid, we_gu, we_down, tmx):
    d = L.D
    d_ff = we_down.shape[1]
    tf = 512
    n_ch = d_ff // tf
    n_tiles = xs.shape[0] // (tmx * SUBLANES)

    def chunk(c, tv, t):
        return jnp.where(tv[t] > 0, c, n_ch - 1)

    grid_spec = pltpu.PrefetchScalarGridSpec(
        num_scalar_prefetch=2,
        grid=(n_tiles, n_ch),
        in_specs=[pl.BlockSpec((tmx * SUBLANES, LANES), lambda t, c, te, tv: (t, 0)),
                  pl.BlockSpec((None, d, tf), lambda t, c, te, tv: (te[t], 0, chunk(c, tv, t))),
                  pl.BlockSpec((None, d, tf), lambda t, c, te, tv: (te[t], 0, n_ch + chunk(c, tv, t))),
                  pl.BlockSpec((None, tf, d), lambda t, c, te, tv: (te[t], chunk(c, tv, t), 0))],
        out_specs=pl.BlockSpec((tmx * SUBLANES, LANES), lambda t, c, te, tv: (t, 0)),
        scratch_shapes=[pltpu.VMEM((tmx, d), BF16), pltpu.VMEM((tmx, d), F32)],
    )
    return pl.pallas_call(
        functools.partial(_experts_kernel, tmx=tmx),
        out_shape=jax.ShapeDtypeStruct(xs.shape, F32),
        grid_spec=grid_spec,
        compiler_params=_params(("arbitrary", "arbitrary")),
        name="moe_experts",
    )(tile_expert, tile_valid, xs, we_gu, we_gu, we_down)


def _combine_kernel(pos_ref, ys_ref, route_ref, x_ref, mod_ref, out_ref, buf1, buf2, sem, *, tmc):
    def copies(t):
        dst1 = _row_block(buf1, t)
        dst2 = _row_block(buf2, t)
        return (pltpu.make_async_copy(_row_block(ys_ref, pos_ref[0, 2 * t]), dst1, sem),
                pltpu.make_async_copy(_row_block(ys_ref, pos_ref[0, 2 * t + 1]), dst2, sem))

    def start(t, carry):
        for cp in copies(t):
            cp.start()
        return carry

    def wait(t, carry):
        for cp in copies(t):
            cp.wait()
        return carry

    lax.fori_loop(0, tmc, start, 0)
    lax.fori_loop(0, tmc, wait, 0)
    w1 = route_ref[:, R_W1:R_W1 + 1]
    w2 = route_ref[:, R_W2:R_W2 + 1]
    for k in range(out_ref.shape[1] // LANES):
        sl = slice(k * LANES, (k + 1) * LANES)
        y = w1 * buf1[pl.ds(k, tmc, stride=SUBLANES), :] + w2 * buf2[pl.ds(k, tmc, stride=SUBLANES), :]
        out_ref[:, sl] = x_ref[:, sl] + mod_ref[G2:G2 + 1, sl] * y


def _combine(L, ys, pos, route, x, mod, with_ctx):
    tmc, d = L.TC, L.D
    ratio = L.tm // tmc
    n_t = L.tiles(with_ctx) * ratio
    return pl.pallas_call(
        functools.partial(_combine_kernel, tmc=tmc),
        out_shape=jax.ShapeDtypeStruct((L.rows(with_ctx), d), F32),
        grid=(n_t,),
        in_specs=[pl.BlockSpec((None, 1, 2 * tmc), lambda i: (i, 0, 0), memory_space=pltpu.SMEM),
                  pl.BlockSpec(memory_space=pl.ANY),
                  pl.BlockSpec((tmc, LANES), lambda i: (i, 0)),
                  pl.BlockSpec((tmc, d), lambda i: (i, 0)),
                  pl.BlockSpec((None, SUBLANES, d), lambda i: (L.mod_group(i // ratio), 0, 0))],
        out_specs=pl.BlockSpec((tmc, d), lambda i: (i, 0)),
        scratch_shapes=[pltpu.VMEM((tmc * SUBLANES, LANES), F32), pltpu.VMEM((tmc * SUBLANES, LANES), F32),
                        pltpu.SemaphoreType.DMA(())],
        compiler_params=_params(("arbitrary",)),
        name="moe_combine",
    )(pos.reshape(-1, 1, 2 * tmc)[:n_t], ys, route, x, mod)


def _moe(L, x, mod, g, w_router, we_gu, we_down, with_ctx):
    tmx = L.tm
    n_tok = L.N if with_ctx else L.NL
    h_rows, route, counts = _router(L, x, mod, g, w_router, with_ctx)
    cnt = counts[0, :N_EXPERTS].astype(jnp.int32)
    tiles_e = (cnt + tmx - 1) // tmx
    tile_end = jnp.cumsum(tiles_e)
    start = (tile_end - tiles_e) * tmx
    n_tiles = (2 * n_tok) // tmx + N_EXPERTS
    tile_id = jnp.arange(n_tiles, dtype=jnp.int32)
    tile_expert = jnp.minimum(jnp.sum(tile_id[:, None] >= tile_end[None, :], axis=1), N_EXPERTS - 1).astype(jnp.int32)
    tile_valid = (tile_id < tile_end[-1]).astype(jnp.int32)
    rt = route[:n_tok]
    sel = jnp.stack([rt[:, R_I1], rt[:, R_I2]], axis=1).astype(jnp.int32)
    rank = jnp.stack([rt[:, R_R1], rt[:, R_R2]], axis=1).astype(jnp.int32)
    pos = (jnp.sum(jnp.where(sel[..., None] == jnp.arange(N_EXPERTS), start, 0), axis=-1) + rank).reshape(-1)
    xs = _dispatch(L, h_rows, pos, n_tiles * tmx, with_ctx)
    ys = _experts(L, xs, tile_expert, tile_valid, we_gu, we_down, tmx)
    return _combine(L, ys, pos, route, x, mod, with_ctx)


def _rope_tables(L, lane_dim, lane_first, lane_freq, lane_is_col, half):
    t = jnp.arange(L.T, dtype=jnp.int32)
    rows = (t // GRID_W).astype(F32)
    cols = (t % GRID_W).astype(F32)
    inv_freq = jnp.exp(-math.log(ROPE_THETA) * jnp.arange(half, dtype=F32) / half)
    pos = jnp.where(lane_is_col[None, :], cols[:, None], rows[:, None])
    ang = pos * inv_freq[lane_freq][None, :]
    cos = jnp.where(lane_dim[None, :], jnp.cos(ang), 1.0)
    sin = jnp.where(lane_dim[None, :], jnp.sin(ang), 0.0)
    sin = jnp.where(lane_first[None, :], -sin, sin)
    ident = jnp.ones((L.tm, LANES), F32)
    return jnp.concatenate([cos, ident], axis=0), jnp.concatenate([sin, 0.0 * ident], axis=0)


def _gqa_rope_tables(L):
    lane = jnp.arange(LANES)
    d = lane % HEAD_DIM
    dd = d % (HEAD_DIM // 2)
    quarter = HEAD_DIM // 4
    return _rope_tables(L, lane >= 0, dd < quarter, dd % quarter, d >= HEAD_DIM // 2, quarter)


def _mla_rope_tables(L):
    lane = jnp.arange(LANES)
    d = lane - NOPE_DIM
    in_rope = (d >= 0) & (d < ROPE_DIM)
    dd = d % (ROPE_DIM // 2)
    quarter = ROPE_DIM // 4
    return _rope_tables(L, in_rope, dd < quarter, dd % quarter, d >= ROPE_DIM // 2, quarter)


def _segment_matrix(bounds):
    lane = jnp.arange(LANES)
    m = jnp.zeros((LANES, LANES), F32)
    for lo, hi in bounds:
        inside = (lane >= lo) & (lane < hi)
        m = m + jnp.where(inside[:, None] & inside[None, :], 1.0 / (hi - lo), 0.0)
    return m.astype(BF16)


def _natten_bias(rel_bias):
    n_h = rel_bias.shape[0]
    col = jnp.arange(GRID_W, dtype=jnp.int32)
    c0 = jnp.clip(col - WIN_C // 2, 0, GRID_W - WIN_C)
    col_ok = (col[None, :] >= c0[:, None]) & (col[None, :] < c0[:, None] + WIN_C)
    col_idx = jnp.clip(col[None, :] - col[:, None] + WIN_C - 1, 0, 2 * WIN_C - 2)
    out = []
    for d in range(WIN_R):
        row_idx = jnp.arange(WIN_R, dtype=jnp.int32) - d + WIN_R - 1
        b = rel_bias[:, row_idx][:, :, col_idx]
        b = jnp.where(col_ok[None, None], b, NEG_INF)
        out.append(b.transpose(0, 2, 1, 3).reshape(n_h, GRID_W, WIN_R * GRID_W))
    return jnp.stack(out, axis=1).astype(F32)


def _dup_heads(w, n_heads):
    k = w.shape[0]
    w = w.reshape(k, n_heads, HEAD_DIM)
    return jnp.concatenate([w, w], axis=-1).reshape(k, n_heads * LANES)


def _gqa_mixer(L, x, mod, g_mix, p, need_ctx, tables):
    w = p["w_qkv"]
    nq = A_HEADS * HEAD_DIM
    nkv = A_KV_HEADS * HEAD_DIM
    w_p = jnp.concatenate([w[:, :nq], _dup_heads(w[:, nq:nq + nkv], A_KV_HEADS),
                           _dup_heads(w[:, nq + nkv:], A_KV_HEADS)], axis=1)
    n_dup = A_KV_HEADS * LANES
    gain = jnp.concatenate([jnp.tile(p["q_gain"], A_HEADS) * HEAD_DIM ** -0.5,
                            jnp.tile(p["k_gain"], 2 * A_KV_HEADS), jnp.ones((n_dup,), F32)]).reshape(1, -1)
    seg = _segment_matrix([(0, HEAD_DIM), (HEAD_DIM, LANES)])
    tn = 512
    qkv = _proj(L, "gqa_qkv", x, w_p, tn=tn, mod=mod, g=g_mix, seg=seg, gain=gain,
                n_norm=(nq + n_dup) // tn, rope=(tables["gqa_cos"], tables["gqa_sin"], HEAD_DIM // 4))
    q_blocks = nq // LANES
    k_blocks = n_dup // LANES
    rep_pairs = N_PAIRS // A_KV_HEADS
    return _attention(L, "gqa_attn", qkv, [lambda p_: p_], qkv, [lambda p_: q_blocks + p_ // rep_pairs],
                      qkv, lambda p_: q_blocks + k_blocks + p_ // rep_pairs, pair=True,
                      mode="all" if need_ctx else "lat")


def _natten_mixer(L, x, mod, g_mix, p, need_ctx):
    n = B_HEADS * HEAD_DIM
    gain = jnp.concatenate([jnp.tile(p["q_gain"], B_HEADS) * HEAD_DIM ** -0.5, jnp.tile(p["k_gain"], B_HEADS),
                            jnp.ones((n,), F32)]).reshape(1, -1)
    seg = _segment_matrix([(0, HEAD_DIM), (HEAD_DIM, LANES)])
    tn = 512
    qkv = _proj(L, "nat_qkv", x, p["w_qkv"], tn=tn, mod=mod, g=g_mix, seg=seg, gain=gain, n_norm=2 * n // tn)
    o = _natten(L, qkv, _natten_bias(p["rel_bias"]))
    if need_ctx:
        o_ctx = _attention(L, "nat_ctx_attn", qkv, [lambda p_: p_], qkv, [lambda p_: N_PAIRS + p_],
                           qkv, lambda p_: 2 * N_PAIRS + p_, pair=True, mode="ctx")
        o = jnp.concatenate([o, o_ctx], axis=0)
    return o


def _mla_mixer(L, x, mod, g_mix, p, need_ctx, tables):
    d = L.D
    w_in = p["w_in"]
    n_c = Q_LORA + KV_LORA
    w_in_p = jnp.concatenate([w_in[:, :n_c], jnp.zeros((d, NOPE_DIM), F32), w_in[:, n_c:],
                              jnp.zeros((d, LANES - NOPE_DIM - ROPE_DIM), F32)], axis=1)
    dq = NOPE_DIM + ROPE_DIM
    pad_q = jnp.zeros((LANES - dq,), F32)
    seg_rope = _segment_matrix([(NOPE_DIM, dq)])
    gain_kr = jnp.concatenate([jnp.zeros((NOPE_DIM,), F32), p["k_gain"][NOPE_DIM:], pad_q]).reshape(1, LANES)
    cos, sin = tables["mla_cos"], tables["mla_sin"]
    cq, ckv, kr = _mla_in(L, x, mod, g_mix, w_in_p, p["g_dq"], p["g_dkv"], seg_rope, gain_kr, cos, sin)

    w_uq = jnp.pad(p["w_uq"].reshape(Q_LORA, C_HEADS, dq), ((0, 0), (0, 0), (0, LANES - dq)))
    gain_q = jnp.tile(jnp.concatenate([p["q_gain"], pad_q]) * dq ** -0.5, C_HEADS).reshape(1, -1)
    seg_q = _segment_matrix([(0, NOPE_DIM), (NOPE_DIM, dq)])
    n_hl = C_HEADS * LANES
    tn = 512
    q = _proj(L, "mla_q", cq, w_uq.reshape(Q_LORA, n_hl), tn=tn, seg=seg_q, gain=gain_q, n_norm=n_hl // tn,
              rope=(cos, sin, ROPE_DIM // 4))

    w_ukv = p["w_ukv"].reshape(KV_LORA, C_HEADS, NOPE_DIM + V_DIM)
    w_uk = jnp.pad(w_ukv[:, :, :NOPE_DIM], ((0, 0), (0, 0), (0, LANES - NOPE_DIM))).reshape(KV_LORA, n_hl)
    w_uv = w_ukv[:, :, NOPE_DIM:].reshape(KV_LORA, C_HEADS * V_DIM)
    gain_k = jnp.tile(jnp.concatenate([p["k_gain"][:NOPE_DIM], jnp.zeros((LANES - NOPE_DIM,), F32)]),
                      C_HEADS).reshape(1, -1)
    seg_k = _segment_matrix([(0, NOPE_DIM)])
    k = _proj(L, "mla_k", ckv, w_uk, tn=tn, seg=seg_k, gain=gain_k, n_norm=n_hl // tn, add=kr)
    v = _proj(L, "mla_v", ckv, w_uv, tn=tn)
    return _attention(L, "mla_attn", q, [lambda p_: 2 * p_, lambda p_: 2 * p_ + 1],
                      k, [lambda p_: 2 * p_, lambda p_: 2 * p_ + 1], v, lambda p_: p_, pair=False,
                      mode="all" if need_ctx else "lat")


def kernel(x, c, ctx, c_ctx, l0_w_mod, l0_b_mod, l0_g_mix, l0_g_ffn, l0_w_qkv, l0_q_gain, l0_k_gain, l0_w_o, l0_w_gu, l0_w_down, l1_w_mod, l1_b_mod, l1_g_mix, l1_g_ffn, l1_w_qkv, l1_q_gain, l1_k_gain, l1_rel_bias, l1_w_o, l1_w_router, l1_we_gu, l1_we_down, l2_w_mod, l2_b_mod, l2_g_mix, l2_g_ffn, l2_w_in, l2_g_dq, l2_g_dkv, l2_w_uq, l2_w_ukv, l2_q_gain, l2_k_gain, l2_w_o, l2_w_gu, l2_w_down, l3_w_mod, l3_b_mod, l3_g_mix, l3_g_ffn, l3_w_qkv, l3_q_gain, l3_k_gain, l3_w_o, l3_w_router, l3_we_gu, l3_we_down):
    layers = [
        dict(w_mod=l0_w_mod, b_mod=l0_b_mod, g_mix=l0_g_mix, g_ffn=l0_g_ffn, kind="gqa",
             mix=dict(w_qkv=l0_w_qkv, q_gain=l0_q_gain, k_gain=l0_k_gain), w_o=l0_w_o,
             ffn=dict(w_gu=l0_w_gu, w_down=l0_w_down)),
        dict(w_mod=l1_w_mod, b_mod=l1_b_mod, g_mix=l1_g_mix, g_ffn=l1_g_ffn, kind="natten",
             mix=dict(w_qkv=l1_w_qkv, q_gain=l1_q_gain, k_gain=l1_k_gain, rel_bias=l1_rel_bias), w_o=l1_w_o,
             moe=dict(w_router=l1_w_router, we_gu=l1_we_gu, we_down=l1_we_down)),
        dict(w_mod=l2_w_mod, b_mod=l2_b_mod, g_mix=l2_g_mix, g_ffn=l2_g_ffn, kind="mla",
             mix=dict(w_in=l2_w_in, g_dq=l2_g_dq, g_dkv=l2_g_dkv, w_uq=l2_w_uq, w_ukv=l2_w_ukv,
                      q_gain=l2_q_gain, k_gain=l2_k_gain), w_o=l2_w_o,
             ffn=dict(w_gu=l2_w_gu, w_down=l2_w_down)),
        dict(w_mod=l3_w_mod, b_mod=l3_b_mod, g_mix=l3_g_mix, g_ffn=l3_g_ffn, kind="gqa",
             mix=dict(w_qkv=l3_w_qkv, q_gain=l3_q_gain, k_gain=l3_k_gain), w_o=l3_w_o,
             moe=dict(w_router=l3_w_router, we_gu=l3_we_gu, we_down=l3_we_down)),
    ]
    batch, seq, d_model = x.shape
    L = _Layout(batch, seq, ctx.shape[1], d_model)
    gqa_cos, gqa_sin = _gqa_rope_tables(L)
    mla_cos, mla_sin = _mla_rope_tables(L)
    tables = dict(gqa_cos=gqa_cos, gqa_sin=gqa_sin, mla_cos=mla_cos, mla_sin=mla_sin)
    cvec = jnp.concatenate([c, c_ctx[None, :], jnp.zeros((SUBLANES - batch - 1, d_model), F32)], axis=0)
    xs = jnp.concatenate([x.reshape(L.NL, d_model), ctx.reshape(L.NC, d_model)], axis=0)
    for li, p in enumerate(layers):
        need_ctx = li < len(layers) - 1
        mod = _adaln(cvec, p["w_mod"], p["b_mod"], batch + 1)
        if p["kind"] == "gqa":
            o = _gqa_mixer(L, xs, mod, p["g_mix"], p["mix"], need_ctx, tables)
        elif p["kind"] == "natten":
            o = _natten_mixer(L, xs, mod, p["g_mix"], p["mix"], need_ctx)
        else:
            o = _mla_mixer(L, xs, mod, p["g_mix"], p["mix"], need_ctx, tables)
        xs = _oproj(L, o, p["w_o"], xs, mod, need_ctx)
        if "ffn" in p:
            xs = _ffn(L, xs, mod, p["g_ffn"], p["ffn"]["w_gu"], p["ffn"]["w_down"], need_ctx)
        else:
            xs = _moe(L, xs, mod, p["g_ffn"], p["moe"]["w_router"], p["moe"]["we_gu"], p["moe"]["we_down"], need_ctx)
    return xs[:L.NL].reshape(batch, seq, d_model)
```

```python
import functools
import math

import jax
import jax.numpy as jnp
from jax import lax
from jax.experimental import pallas as pl
from jax.experimental.pallas import tpu as pltpu

F32 = jnp.float32
BF16 = jnp.bfloat16

LANES = 128
SUBLANES = 8
VMEM_LIMIT_BYTES = 56 * 1024 * 1024

NORM_EPS = 1e-6
ROPE_THETA = 10000.0
NEG_INF = -1e30
GRID_W = 64
HEAD_DIM = 64
WIN_R = 8
WIN_C = 16
A_HEADS = 16
A_KV_HEADS = 4
B_HEADS = 16
C_HEADS = 16
Q_LORA = 384
KV_LORA = 256
NOPE_DIM = 64
ROPE_DIM = 32
V_DIM = 64
N_EXPERTS = 8
N_PAIRS = 8
LOG2E = math.log2(math.e)
FAST_BOUND = 40.0
BOUND_MARGIN = 1.02
ONES_LANE = (HEAD_DIM, 0)

SH1, SC1, G1, SH2, SC2, G2 = range(6)


def _params(sem):
    return pltpu.CompilerParams(dimension_semantics=sem, vmem_limit_bytes=VMEM_LIMIT_BYTES)


def _dot(a, b):
    return jnp.dot(a, b, preferred_element_type=F32)


def _dot_nt(a, b):
    return lax.dot_general(a, b, (((1,), (1,)), ((), ())), preferred_element_type=F32)


def _silu(x):
    return x * (1.0 / (1.0 + jnp.exp(-x)))


def _norm_mod(x, g, shift, scale):
    ms = jnp.mean(x * x, axis=-1, keepdims=True)
    y = x * lax.rsqrt(ms + NORM_EPS) * g
    return y * (1.0 + scale) + shift


def _seg_mean_sq(y, seg):
    sq = y * y
    hi = sq.astype(BF16)
    lo = (sq - hi.astype(F32)).astype(BF16)
    return _dot(hi, seg) + _dot(lo, seg)


def _rope(y, cos, sin_signed, shift):
    lane = lax.broadcasted_iota(jnp.int32, y.shape, 1)
    first = (lane % (2 * shift)) < shift
    partner = jnp.where(first, pltpu.roll(y, LANES - shift, 1), pltpu.roll(y, shift, 1))
    return y * cos + partner * sin_signed


class _Layout:
    def __init__(self, batch, seq, ctx_len, d_model):
        self.B, self.T, self.TC, self.D = batch, seq, ctx_len, d_model
        self.NL = batch * seq
        self.NC = batch * ctx_len
        self.N = self.NL + self.NC
        self.tm = 1024 if (self.NC % 1024 == 0 and seq % 1024 == 0) else ctx_len
        assert seq % self.tm == 0 and self.NC % self.tm == 0
        self.tpb = seq // self.tm
        self.tq = ctx_len
        assert seq % self.tq == 0 and seq % GRID_W == 0

    def mod_group(self, i):
        return jnp.minimum(i // self.tpb, self.B)

    def rope_block(self, i):
        return jnp.where(i < self.B * self.tpb, i % self.tpb, self.tpb)

    def rows(self, with_ctx):
        return self.N if with_ctx else self.NL

    def tiles(self, with_ctx):
        return self.rows(with_ctx) // self.tm


def _adaln_kernel(c_ref, w_ref, b_ref, o_ref):
    s = _silu(c_ref[...])
    o_ref[...] = jnp.dot(s, w_ref[...], preferred_element_type=F32,
                         precision=lax.Precision.HIGHEST) + b_ref[...]


def _adaln(cvec, w_mod, b_mod, n_groups):
    d, n_out = w_mod.shape
    tn = n_out // 4
    out = pl.pallas_call(
        _adaln_kernel,
        out_shape=jax.ShapeDtypeStruct((SUBLANES, n_out), F32),
        grid=(n_out // tn,),
        in_specs=[pl.BlockSpec((SUBLANES, d), lambda j: (0, 0)),
                  pl.BlockSpec((d, tn), lambda j: (0, j)),
                  pl.BlockSpec((1, tn), lambda j: (0, j))],
        out_specs=pl.BlockSpec((SUBLANES, tn), lambda j: (0, j)),
        compiler_params=_params(("arbitrary",)),
        name="adaln",
    )(cvec, w_mod, b_mod.reshape(1, n_out))
    mod = out[:n_groups].reshape(n_groups, 6, d)
    return jnp.pad(mod, ((0, 0), (0, 2), (0, 0)))


def _proj_kernel(*refs, has_mod, n_norm, n_col_tiles, rope_shift, has_add, has_colbias, groups):
    refs = list(refs)
    x_ref = refs.pop(0)
    if has_mod:
        mod_ref, g_ref = refs.pop(0), refs.pop(0)
    w_ref = refs.pop(0)
    if n_norm:
        seg_ref, gain_ref = refs.pop(0), refs.pop(0)
    if rope_shift:
        cos_ref, sin_ref = refs.pop(0), refs.pop(0)
    if has_add:
        add_ref = refs.pop(0)
    if has_colbias:
        colbias_ref = refs.pop(0)
    o_ref = refs.pop(0)
    j = pl.program_id(1)

    if has_mod:
        h_scr = refs.pop(0)

        @pl.when(j == 0)
        def _():
            h = _norm_mod(x_ref[...], g_ref[...], mod_ref[SH1:SH1 + 1, :], mod_ref[SC1:SC1 + 1, :])
            h_scr[...] = h.astype(BF16)

        h = h_scr[...]
    else:
        h = x_ref[...]
    acc = _dot(h, w_ref[...].astype(BF16))

    def normed():
        for c in range(groups):
            sl = slice(c * LANES, (c + 1) * LANES)
            y = acc[:, sl]
            y = y * lax.rsqrt(_seg_mean_sq(y, seg_ref[...]) + NORM_EPS) * gain_ref[:, sl]
            if rope_shift:
                y = _rope(y, cos_ref[...], sin_ref[...], rope_shift)
            if has_add:
                y = y + add_ref[...]
            o_ref[:, sl] = y.astype(o_ref.dtype)

    def plain():
        y = acc + colbias_ref[...] if has_colbias else acc
        o_ref[...] = y.astype(o_ref.dtype)

    if n_norm == 0:
        plain()
    elif n_norm == n_col_tiles:
        normed()
    else:
        pl.when(j < n_norm)(normed)
        pl.when(j >= n_norm)(plain)


def _proj(L, name, x, w, *, with_ctx=True, tn=512, mod=None, g=None, seg=None, gain=None, n_norm=0,
          rope=None, add=None, colbias=None, out_dtype=BF16):
    k, n_out = w.shape
    tm = L.tm
    n_col = n_out // tn
    in_specs = [pl.BlockSpec((tm, k), lambda i, j: (i, 0))]
    args = [x]
    scratch = []
    if mod is not None:
        in_specs += [pl.BlockSpec((None, SUBLANES, k), lambda i, j: (L.mod_group(i), 0, 0)),
                     pl.BlockSpec((1, k), lambda i, j: (0, 0))]
        args += [mod, g.reshape(1, k)]
        scratch.append(pltpu.VMEM((tm, k), BF16))
    in_specs.append(pl.BlockSpec((k, tn), lambda i, j: (0, j)))
    args.append(w)
    if n_norm:
        in_specs += [pl.BlockSpec((LANES, LANES), lambda i, j: (0, 0)),
                     pl.BlockSpec((1, tn), lambda i, j: (0, j))]
        args += [seg, gain]
    rope_shift = 0
    if rope is not None:
        cos, sin, rope_shift = rope
        in_specs += [pl.BlockSpec((tm, LANES), lambda i, j: (L.rope_block(i), 0))] * 2
        args += [cos, sin]
    if add is not None:
        in_specs.append(pl.BlockSpec((tm, LANES), lambda i, j: (i, 0)))
        args.append(add)
    if colbias is not None:
        in_specs.append(pl.BlockSpec((1, tn), lambda i, j: (0, j)))
        args.append(colbias)
    kern = functools.partial(_proj_kernel, has_mod=mod is not None, n_norm=n_norm, n_col_tiles=n_col,
                             rope_shift=rope_shift, has_add=add is not None, has_colbias=colbias is not None,
                             groups=tn // LANES)
    return pl.pallas_call(
        kern,
        out_shape=jax.ShapeDtypeStruct((L.N, n_out), out_dtype),
        grid=(L.tiles(with_ctx), n_col),
        in_specs=in_specs,
        out_specs=pl.BlockSpec((tm, tn), lambda i, j: (i, j)),
        scratch_shapes=scratch,
        compiler_params=_params(("parallel", "arbitrary")),
        name=name,
    )(*args)


def _mla_in_kernel(x_ref, mod_ref, g_ref, w_ref, gdq_ref, gdkv_ref, seg_ref, gain_ref, cos_ref, sin_ref,
                   cq_ref, ckv_ref, kr_ref):
    h = _norm_mod(x_ref[...], g_ref[...], mod_ref[SH1:SH1 + 1, :], mod_ref[SC1:SC1 + 1, :])
    acc = _dot(h.astype(BF16), w_ref[...].astype(BF16))
    cq = acc[:, :Q_LORA]
    cq = cq * lax.rsqrt(jnp.mean(cq * cq, axis=-1, keepdims=True) + NORM_EPS) * gdq_ref[...]
    cq_ref[...] = cq.astype(BF16)
    ckv = acc[:, Q_LORA:Q_LORA + KV_LORA]
    ckv = ckv * lax.rsqrt(jnp.mean(ckv * ckv, axis=-1, keepdims=True) + NORM_EPS) * gdkv_ref[...]
    ckv_ref[...] = ckv.astype(BF16)
    kr = acc[:, Q_LORA + KV_LORA:]
    kr = kr * lax.rsqrt(_seg_mean_sq(kr, seg_ref[...]) + NORM_EPS) * gain_ref[...]
    kr_ref[...] = _rope(kr, cos_ref[...], sin_ref[...], ROPE_DIM // 4)


def _mla_in(L, x, mod, g, w_in_p, g_dq, g_dkv, seg_kr, gain_kr, cos, sin):
    tm, d = L.tm, L.D
    n_out = w_in_p.shape[1]
    full = lambda shape: pl.BlockSpec(shape, lambda i: (0,) * len(shape))
    return pl.pallas_call(
        _mla_in_kernel,
        out_shape=(jax.ShapeDtypeStruct((L.N, Q_LORA), BF16),
                   jax.ShapeDtypeStruct((L.N, KV_LORA), BF16),
                   jax.ShapeDtypeStruct((L.N, LANES), F32)),
        grid=(L.tiles(True),),
        in_specs=[pl.BlockSpec((tm, d), lambda i: (i, 0)),
                  pl.BlockSpec((None, SUBLANES, d), lambda i: (L.mod_group(i), 0, 0)),
                  full((1, d)), full((d, n_out)), full((1, Q_LORA)), full((1, KV_LORA)),
                  full((LANES, LANES)), full((1, LANES)),
                  pl.BlockSpec((tm, LANES), lambda i: (L.rope_block(i), 0)),
                  pl.BlockSpec((tm, LANES), lambda i: (L.rope_block(i), 0))],
        out_specs=(pl.BlockSpec((tm, Q_LORA), lambda i: (i, 0)),
                   pl.BlockSpec((tm, KV_LORA), lambda i: (i, 0)),
                   pl.BlockSpec((tm, LANES), lambda i: (i, 0))),
        compiler_params=_params(("parallel",)),
        name="mla_in",
    )(x, mod, g.reshape(1, d), w_in_p, g_dq.reshape(1, -1), g_dkv.reshape(1, -1), seg_kr, gain_kr, cos, sin)


def _softmax_pv(s_list, v_list, bound, ones_lane, fast):
    if fast:
        o = None
        for s, v in zip(s_list, v_list):
            part = _dot(jnp.exp2(s - bound).astype(BF16), v)
            o = part if o is None else o + part
        return o / o[:, ones_lane:ones_lane + 1]
    m = None
    for s in s_list:
        ms = jnp.max(s, axis=-1, keepdims=True)
        m = ms if m is None else jnp.maximum(m, ms)
    o, den = None, None
    for s, v in zip(s_list, v_list):
        p = jnp.exp2(s - m)
        ds = jnp.sum(p, axis=-1, keepdims=True)
        part = _dot(p.astype(BF16), v)
        o = part if o is None else o + part
        den = ds if den is None else den + ds
    return o / den


def _attn_kernel(*refs, nq, nk, pair, lat_keys, nq_lat, ctx_q):
    refs = list(refs)
    bound_ref = refs.pop(0)
    q_refs = [refs.pop(0) for _ in range(nq)]
    kl_refs = [refs.pop(0) for _ in range(nk)] if lat_keys else []
    kc_refs = [refs.pop(0) for _ in range(nk)]
    vl_refs = [refs.pop(0) for _ in range(2)] if lat_keys else []
    vc_refs = [refs.pop(0) for _ in range(2)]
    o_ref = refs.pop(0)
    lo = lax.broadcasted_iota(jnp.int32, (1, LANES), 1) < HEAD_DIM
    bound = bound_ref[0, 0]

    def run(use_lat, fast):
        outs = []
        for hh in range(2):
            q = q_refs[hh % nq][...]
            if pair:
                q = jnp.where(lo if hh == 0 else jnp.logical_not(lo), q, jnp.zeros_like(q))
            s_list = [_dot_nt(q, kc_refs[hh % nk][...])]
            v_list = [vc_refs[hh][...]]
            if use_lat:
                s_list.append(_dot_nt(q, kl_refs[hh % nk][...]))
                v_list.append(vl_refs[hh][...])
            outs.append(_softmax_pv(s_list, v_list, bound, ONES_LANE[hh], fast))
        o_ref[...] = jnp.where(lo, outs[0], outs[1]).astype(o_ref.dtype)

    is_fast = bound <= FAST_BOUND
    for fast in (True, False):
        pred = is_fast if fast else jnp.logical_not(is_fast)
        if lat_keys and ctx_q:
            i = pl.program_id(2)
            pl.when(jnp.logical_and(pred, i < nq_lat))(functools.partial(run, True, fast))
            pl.when(jnp.logical_and(pred, i == nq_lat))(functools.partial(run, False, fast))
        else:
            pl.when(pred)(functools.partial(run, lat_keys, fast))


def _attention(L, name, bound, q, q_cols, k, k_cols, v, v_cols, *, pair, mode):
    B, T, TC, tq = L.B, L.T, L.TC, L.tq
    nq_lat = T // tq
    ctx_row = L.NL // TC
    lat_keys = mode != "ctx"
    ctx_q = mode != "lat"
    n_i = {"all": nq_lat + 1, "lat": nq_lat, "ctx": 1}[mode]

    def q_row(b, i):
        if mode == "ctx":
            return ctx_row + b
        if mode == "lat":
            return b * nq_lat + i
        return jnp.where(i < nq_lat, b * nq_lat + i, ctx_row + b)

    in_specs = [pl.BlockSpec(memory_space=pltpu.SMEM)]
    args = [bound]
    for f in q_cols:
        in_specs.append(pl.BlockSpec((tq, LANES), lambda b, p, i, f=f: (q_row(b, i), f(p))))
        args.append(q)
    if lat_keys:
        for f in k_cols:
            in_specs.append(pl.BlockSpec((T, LANES), lambda b, p, i, f=f: (b, f(p))))
            args.append(k)
    for f in k_cols:
        in_specs.append(pl.BlockSpec((TC, LANES), lambda b, p, i, f=f: (ctx_row + b, f(p))))
        args.append(k)
    if lat_keys:
        for f in v_cols:
            in_specs.append(pl.BlockSpec((T, LANES), lambda b, p, i, f=f: (b, f(p))))
            args.append(v)
    for f in v_cols:
        in_specs.append(pl.BlockSpec((TC, LANES), lambda b, p, i, f=f: (ctx_row + b, f(p))))
        args.append(v)
    out_rows = {"all": L.N, "lat": L.NL, "ctx": L.NC}[mode]
    out_row = (lambda b, i: b) if mode == "ctx" else q_row
    kern = functools.partial(_attn_kernel, nq=len(q_cols), nk=len(k_cols), pair=pair, lat_keys=lat_keys,
                             nq_lat=nq_lat, ctx_q=ctx_q)
    return pl.pallas_call(
        kern,
        out_shape=jax.ShapeDtypeStruct((out_rows, N_PAIRS * LANES), BF16),
        grid=(B, N_PAIRS, n_i),
        in_specs=in_specs,
        out_specs=pl.BlockSpec((tq, LANES), lambda b, p, i: (out_row(b, i), p)),
        compiler_params=_params(("parallel", "parallel", "arbitrary")),
        name=name,
    )(*args)


def _natten_kernel(bound_ref, q_ref, k_ref, va_ref, vb_ref, kc_ref, vca_ref, vcb_ref, bias_ref, o_ref, oc_scr, *,
                   rows_n):
    lo = lax.broadcasted_iota(jnp.int32, (1, LANES), 1) < HEAD_DIM
    halves = (lo, jnp.logical_not(lo))
    v_refs = (va_ref, vb_ref)
    vc_refs = (vca_ref, vcb_ref)
    win = WIN_R * GRID_W
    bound = bound_ref[0, 0]

    def window(r):
        r0 = jnp.clip(r - WIN_R // 2, 0, rows_n - WIN_R)
        rows = pl.ds(pl.multiple_of(r * GRID_W, GRID_W), GRID_W)
        keys = pl.ds(pl.multiple_of(r0 * GRID_W, GRID_W), win)
        return r - r0, rows, keys

    def masked(q, hh):
        return jnp.where(halves[hh], q, jnp.zeros_like(q))

    def fast_body(r, carry):
        d, rows, keys = window(r)
        q = q_ref[rows, :]
        kw = k_ref[keys, :]
        outs = []
        for hh in range(2):
            s_l = _dot_nt(masked(q, hh), kw) + bias_ref[hh, d]
            o = oc_scr[hh, rows, :] + _dot(jnp.exp2(s_l - bound).astype(BF16), v_refs[hh][keys, :])
            outs.append(o / o[:, ONES_LANE[hh]:ONES_LANE[hh] + 1])
        o_ref[rows, :] = jnp.where(lo, outs[0], outs[1]).astype(o_ref.dtype)
        return carry

    def exact_body(r, carry):
        d, rows, keys = window(r)
        q = q_ref[rows, :]
        kw = k_ref[keys, :]
        outs = []
        for hh in range(2):
            qh = masked(q, hh)
            s_l = _dot_nt(qh, kw) + bias_ref[hh, d]
            s_c = _dot_nt(qh, kc_ref[...])
            outs.append(_softmax_pv([s_l, s_c], [v_refs[hh][keys, :], vc_refs[hh][...]], bound, ONES_LANE[hh], False))
        o_ref[rows, :] = jnp.where(lo, outs[0], outs[1]).astype(o_ref.dtype)
        return carry

    is_fast = bound <= FAST_BOUND

    @pl.when(is_fast)
    def _():
        q_all = q_ref[...]
        for hh in range(2):
            p_c = jnp.exp2(_dot_nt(masked(q_all, hh), kc_ref[...]) - bound).astype(BF16)
            oc_scr[hh] = _dot(p_c, vc_refs[hh][...])
        lax.fori_loop(0, rows_n, fast_body, 0, unroll=4)

    @pl.when(jnp.logical_not(is_fast))
    def _():
        lax.fori_loop(0, rows_n, exact_body, 0)


def _natten(L, bound, qkv, bias):
    B, T, TC = L.B, L.T, L.TC
    ctx_row = L.NL // TC
    win = WIN_R * GRID_W
    k0, v0 = N_PAIRS, 2 * N_PAIRS
    return pl.pallas_call(
        functools.partial(_natten_kernel, rows_n=T // GRID_W),
        out_shape=jax.ShapeDtypeStruct((L.NL, N_PAIRS * LANES), BF16),
        grid=(B, N_PAIRS),
        in_specs=[pl.BlockSpec(memory_space=pltpu.SMEM),
                  pl.BlockSpec((T, LANES), lambda b, p: (b, p)),
                  pl.BlockSpec((T, LANES), lambda b, p: (b, k0 + p)),
                  pl.BlockSpec((T, LANES), lambda b, p: (b, v0 + 2 * p)),
                  pl.BlockSpec((T, LANES), lambda b, p: (b, v0 + 2 * p + 1)),
                  pl.BlockSpec((TC, LANES), lambda b, p: (ctx_row + b, k0 + p)),
                  pl.BlockSpec((TC, LANES), lambda b, p: (ctx_row + b, v0 + 2 * p)),
                  pl.BlockSpec((TC, LANES), lambda b, p: (ctx_row + b, v0 + 2 * p + 1)),
                  pl.BlockSpec((2, WIN_R, GRID_W, win), lambda b, p: (p, 0, 0, 0))],
        out_specs=pl.BlockSpec((T, LANES), lambda b, p: (b, p)),
        scratch_shapes=[pltpu.VMEM((2, T, LANES), F32)],
        compiler_params=_params(("parallel", "parallel")),
        name="natten",
    )(bound, qkv, qkv, qkv, qkv, qkv, qkv, qkv, bias)


def _oproj_kernel(o_ref, w_ref, x_ref, mod_ref, out_ref):
    acc = _dot(o_ref[...], w_ref[...].astype(BF16))
    out_ref[...] = x_ref[...] + mod_ref[G1:G1 + 1, :] * acc


def _oproj(L, o, w_o, x, mod, with_ctx):
    tm, d = L.tm // 2, L.D
    k = w_o.shape[0]
    ratio = L.tm // tm
    return pl.pallas_call(
        _oproj_kernel,
        out_shape=jax.ShapeDtypeStruct((L.rows(with_ctx), d), F32),
        grid=(L.tiles(with_ctx) * ratio,),
        in_specs=[pl.BlockSpec((tm, k), lambda i: (i, 0)),
                  pl.BlockSpec((k, d), lambda i: (0, 0)),
                  pl.BlockSpec((tm, d), lambda i: (i, 0)),
                  pl.BlockSpec((None, SUBLANES, d), lambda i: (L.mod_group(i // ratio), 0, 0))],
        out_specs=pl.BlockSpec((tm, d), lambda i: (i, 0)),
        compiler_params=_params(("parallel",)),
        name="oproj",
    )(o, w_o, x, mod)


def _ffn_kernel(x_ref, mod_ref, g_ref, wg_ref, wu_ref, wd_ref, out_ref, h_scr, acc_scr):
    c = pl.program_id(1)

    @pl.when(c == 0)
    def _():
        h = _norm_mod(x_ref[...], g_ref[...], mod_ref[SH2:SH2 + 1, :], mod_ref[SC2:SC2 + 1, :])
        h_scr[...] = h.astype(BF16)
        acc_scr[...] = jnp.zeros_like(acc_scr)

    h = h_scr[...]
    gate = _dot(h, wg_ref[...].astype(BF16))
    up = _dot(h, wu_ref[...].astype(BF16))
    act = (_silu(gate) * up).astype(BF16)
    acc_scr[...] += _dot(act, wd_ref[...].astype(BF16))

    @pl.when(c == pl.num_programs(1) - 1)
    def _():
        out_ref[...] = x_ref[...] + mod_ref[G2:G2 + 1, :] * acc_scr[...]


def _ffn(L, x, mod, g, w_gu, w_down, with_ctx):
    tm, d = L.tm, L.D
    d_ff = w_down.shape[0]
    tf = 256
    n_ch = d_ff // tf
    return pl.pallas_call(
        _ffn_kernel,
        out_shape=jax.ShapeDtypeStruct((L.rows(with_ctx), d), F32),
        grid=(L.tiles(with_ctx), n_ch),
        in_specs=[pl.BlockSpec((tm, d), lambda i, c: (i, 0)),
                  pl.BlockSpec((None, SUBLANES, d), lambda i, c: (L.mod_group(i), 0, 0)),
                  pl.BlockSpec((1, d), lambda i, c: (0, 0)),
                  pl.BlockSpec((d, tf), lambda i, c: (0, c)),
                  pl.BlockSpec((d, tf), lambda i, c: (0, n_ch + c)),
                  pl.BlockSpec((tf, d), lambda i, c: (c, 0))],
        out_specs=pl.BlockSpec((tm, d), lambda i, c: (i, 0)),
        scratch_shapes=[pltpu.VMEM((tm, d), BF16), pltpu.VMEM((tm, d), F32)],
        compiler_params=_params(("parallel", "arbitrary")),
        name="ffn",
    )(x, mod, g.reshape(1, d), w_gu, w_gu, w_down)


R_I1, R_I2, R_W1, R_W2, R_R1, R_R2 = range(6)


def _router_kernel(x_ref, mod_ref, g_ref, wr_ref, route_ref, cnt_ref, carry_scr, *, tm):
    i = pl.program_id(0)

    @pl.when(i == 0)
    def _():
        carry_scr[...] = jnp.zeros_like(carry_scr)

    h = _norm_mod(x_ref[...], g_ref[...], mod_ref[SH2:SH2 + 1, :], mod_ref[SC2:SC2 + 1, :])
    logits = jnp.dot(h, wr_ref[...], preferred_element_type=F32, precision=lax.Precision.HIGHEST)
    lane = lax.broadcasted_iota(jnp.int32, logits.shape, 1).astype(F32)
    lg = jnp.where(lane < N_EXPERTS, logits, -jnp.inf)
    m1 = jnp.max(lg, axis=-1, keepdims=True)
    i1 = jnp.min(jnp.where(lg == m1, lane, float(LANES)), axis=-1, keepdims=True)
    lg2 = jnp.where(lane == i1, -jnp.inf, lg)
    m2 = jnp.max(lg2, axis=-1, keepdims=True)
    i2 = jnp.min(jnp.where(lg2 == m2, lane, float(LANES)), axis=-1, keepdims=True)
    e = jnp.exp(m2 - m1)
    w1 = 1.0 / (1.0 + e)
    w2 = e / (1.0 + e)
    onehot = jnp.where((lane == i1) | (lane == i2), 1.0, 0.0)
    row = lax.broadcasted_iota(jnp.int32, (tm, tm), 0)
    col = lax.broadcasted_iota(jnp.int32, (tm, tm), 1)
    tri = jnp.where(row > col, 1.0, 0.0).astype(BF16)
    rank = _dot(tri, onehot.astype(BF16)) + carry_scr[0:1, :]
    r1 = jnp.sum(jnp.where(lane == i1, rank, 0.0), axis=-1, keepdims=True)
    r2 = jnp.sum(jnp.where(lane == i2, rank, 0.0), axis=-1, keepdims=True)
    carry_scr[...] = carry_scr[...] + jnp.sum(onehot, axis=0, keepdims=True)
    rec = jnp.zeros_like(logits)
    for pos, val in ((R_I1, i1), (R_I2, i2), (R_W1, w1), (R_W2, w2), (R_R1, r1), (R_R2, r2)):
        rec = jnp.where(lane == float(pos), val, rec)
    route_ref[...] = rec
    cnt_ref[...] = carry_scr[...]


def _router(L, x, mod, g, w_router, with_ctx):
    tm, d = L.tm, L.D
    wr = jnp.pad(w_router, ((0, 0), (0, LANES - w_router.shape[1])))
    return pl.pallas_call(
        functools.partial(_router_kernel, tm=tm),
        out_shape=(jax.ShapeDtypeStruct((L.rows(with_ctx), LANES), F32),
                   jax.ShapeDtypeStruct((SUBLANES, LANES), F32)),
        grid=(L.tiles(with_ctx),),
        in_specs=[pl.BlockSpec((tm, d), lambda i: (i, 0)),
                  pl.BlockSpec((None, SUBLANES, d), lambda i: (L.mod_group(i), 0, 0)),
                  pl.BlockSpec((1, d), lambda i: (0, 0)),
                  pl.BlockSpec((d, LANES), lambda i: (0, 0))],
        out_specs=(pl.BlockSpec((tm, LANES), lambda i: (i, 0)),
                   pl.BlockSpec((SUBLANES, LANES), lambda i: (0, 0))),
        scratch_shapes=[pltpu.VMEM((SUBLANES, LANES), F32)],
        compiler_params=_params(("arbitrary",)),
        name="router",
    )(x, mod, g.reshape(1, d), wr)


def _row_block(ref, row):
    return ref.at[pl.ds(pl.multiple_of(row * SUBLANES, SUBLANES), SUBLANES)]


def _dispatch_kernel(pos_ref, x_ref, mod_ref, g_ref, xs_in_ref, xs_ref, rows_scr, sem, *, tmd):
    del xs_in_ref
    h = _norm_mod(x_ref[...], g_ref[...], mod_ref[SH2:SH2 + 1, :], mod_ref[SC2:SC2 + 1, :])
    for c in range(h.shape[1] // LANES):
        rows_scr[pl.ds(c, tmd, stride=SUBLANES), :] = h[:, c * LANES:(c + 1) * LANES]

    def copies(t):
        src = _row_block(rows_scr, t)
        return (pltpu.make_async_copy(src, _row_block(xs_ref, pos_ref[0, 2 * t]), sem),
                pltpu.make_async_copy(src, _row_block(xs_ref, pos_ref[0, 2 * t + 1]), sem))

    def start(t, carry):
        for cp in copies(t):
            cp.start()
        return carry

    def wait(t, carry):
        for cp in copies(t):
            cp.wait()
        return carry

    lax.fori_loop(0, tmd, start, 0)
    lax.fori_loop(0, tmd, wait, 0)


def _dispatch(L, x, mod, g, pos, n_slots, with_ctx):
    tmd, d = L.tm, L.D
    n_t = L.tiles(with_ctx)
    xs0 = jnp.zeros((n_slots * SUBLANES, LANES), F32)
    return pl.pallas_call(
        functools.partial(_dispatch_kernel, tmd=tmd),
        out_shape=jax.ShapeDtypeStruct(xs0.shape, F32),
        grid=(n_t,),
        in_specs=[pl.BlockSpec((None, 1, 2 * tmd), lambda i: (i, 0, 0), memory_space=pltpu.SMEM),
                  pl.BlockSpec((tmd, d), lambda i: (i, 0)),
                  pl.BlockSpec((None, SUBLANES, d), lambda i: (L.mod_group(i), 0, 0)),
                  pl.BlockSpec((1, d), lambda i: (0, 0)),
                  pl.BlockSpec(memory_space=pl.ANY)],
        out_specs=pl.BlockSpec(memory_space=pl.ANY),
        scratch_shapes=[pltpu.VMEM((tmd * SUBLANES, LANES), F32), pltpu.SemaphoreType.DMA(())],
        input_output_aliases={4: 0},
        compiler_params=pltpu.CompilerParams(dimension_semantics=("arbitrary",), has_side_effects=True,
                                             vmem_limit_bytes=VMEM_LIMIT_BYTES),
        name="moe_dispatch",
    )(pos.reshape(-1, 1, 2 * tmd)[:n_t], x, mod, g.reshape(1, d), xs0)


def _experts_kernel(te_ref, tv_ref, xs_ref, wg_ref, wu_ref, wd_ref, ys_ref, h_scr, acc_scr, *, tmx):
    del te_ref
    t, c = pl.program_id(0), pl.program_id(1)
    valid = tv_ref[t] > 0
    groups = h_scr.shape[1] // LANES

    @pl.when(jnp.logical_and(valid, c == 0))
    def _():
        for k in range(groups):
            h_scr[:, k * LANES:(k + 1) * LANES] = xs_ref[pl.ds(k, tmx, stride=SUBLANES), :].astype(BF16)
        acc_scr[...] = jnp.zeros_like(acc_scr)

    @pl.when(valid)
    def _():
        h = h_scr[...]
        gate = _dot(h, wg_ref[...].astype(BF16))
        up = _dot(h, wu_ref[...].astype(BF16))
        act = (_silu(gate) * up).astype(BF16)
        acc_scr[...] += _dot(act, wd_ref[...].astype(BF16))

    @pl.when(jnp.logical_and(valid, c == pl.num_programs(1) - 1))
    def _():
        for k in range(groups):
            ys_ref[pl.ds(k, tmx, stride=SUBLANES), :] = acc_scr[:, k * LANES:(k + 1) * LANES]

    @pl.when(jnp.logical_and(jnp.logical_not(valid), c == 0))
    def _():
        ys_ref[...] = jnp.zeros_like(ys_ref)


def _experts(L, xs, tile_expert, tile_valid, we_gu, we_down, tmx):
    d = L.D
    d_ff = we_down.shape[1]
    tf = 512
    n_ch = d_ff // tf
    n_tiles = xs.shape[0] // (tmx * SUBLANES)

    def chunk(c, tv, t):
        return jnp.where(tv[t] > 0, c, n_ch - 1)

    grid_spec = pltpu.PrefetchScalarGridSpec(
        num_scalar_prefetch=2,
        grid=(n_tiles, n_ch),
        in_specs=[pl.BlockSpec((tmx * SUBLANES, LANES), lambda t, c, te, tv: (t, 0)),
                  pl.BlockSpec((None, d, tf), lambda t, c, te, tv: (te[t], 0, chunk(c, tv, t))),
                  pl.BlockSpec((None, d, tf), lambda t, c, te, tv: (te[t], 0, n_ch + chunk(c, tv, t))),
                  pl.BlockSpec((None, tf, d), lambda t, c, te, tv: (te[t], chunk(c, tv, t), 0))],
        out_specs=pl.BlockSpec((tmx * SUBLANES, LANES), lambda t, c, te, tv: (t, 0)),
        scratch_shapes=[pltpu.VMEM((tmx, d), BF16), pltpu.VMEM((tmx, d), F32)],
    )
    return pl.pallas_call(
        functools.partial(_experts_kernel, tmx=tmx),
        out_shape=jax.ShapeDtypeStruct(xs.shape, F32),
        grid_spec=grid_spec,
        compiler_params=_params(("arbitrary", "arbitrary")),
        name="moe_experts",
    )(tile_expert, tile_valid, xs, we_gu, we_gu, we_down)


def _combine_kernel(pos_ref, ys_ref, route_ref, x_ref, mod_ref, out_ref, buf1, buf2, sem, *, tmc):
    def copies(t):
        dst1 = _row_block(buf1, t)
        dst2 = _row_block(buf2, t)
        return (pltpu.make_async_copy(_row_block(ys_ref, pos_ref[0, 2 * t]), dst1, sem),
                pltpu.make_async_copy(_row_block(ys_ref, pos_ref[0, 2 * t + 1]), dst2, sem))

    def start(t, carry):
        for cp in copies(t):
            cp.start()
        return carry

    def wait(t, carry):
        for cp in copies(t):
            cp.wait()
        return carry

    lax.fori_loop(0, tmc, start, 0)
    lax.fori_loop(0, tmc, wait, 0)
    w1 = route_ref[:, R_W1:R_W1 + 1]
    w2 = route_ref[:, R_W2:R_W2 + 1]
    for k in range(out_ref.shape[1] // LANES):
        sl = slice(k * LANES, (k + 1) * LANES)
        y = w1 * buf1[pl.ds(k, tmc, stride=SUBLANES), :] + w2 * buf2[pl.ds(k, tmc, stride=SUBLANES), :]
        out_ref[:, sl] = x_ref[:, sl] + mod_ref[G2:G2 + 1, sl] * y


def _combine(L, ys, pos, route, x, mod, with_ctx):
    tmc, d = L.TC, L.D
    ratio = L.tm // tmc
    n_t = L.tiles(with_ctx) * ratio
    return pl.pallas_call(
        functools.partial(_combine_kernel, tmc=tmc),
        out_shape=jax.ShapeDtypeStruct((L.rows(with_ctx), d), F32),
        grid=(n_t,),
        in_specs=[pl.BlockSpec((None, 1, 2 * tmc), lambda i: (i, 0, 0), memory_space=pltpu.SMEM),
                  pl.BlockSpec(memory_space=pl.ANY),
                  pl.BlockSpec((tmc, LANES), lambda i: (i, 0)),
                  pl.BlockSpec((tmc, d), lambda i: (i, 0)),
                  pl.BlockSpec((None, SUBLANES, d), lambda i: (L.mod_group(i // ratio), 0, 0))],
        out_specs=pl.BlockSpec((tmc, d), lambda i: (i, 0)),
        scratch_shapes=[pltpu.VMEM((tmc * SUBLANES, LANES), F32), pltpu.VMEM((tmc * SUBLANES, LANES), F32),
                        pltpu.SemaphoreType.DMA(())],
        compiler_params=_params(("arbitrary",)),
        name="moe_combine",
    )(pos.reshape(-1, 1, 2 * tmc)[:n_t], ys, route, x, mod)


def _moe(L, x, mod, g, w_router, we_gu, we_down, with_ctx):
    tmx = L.tm
    n_tok = L.N if with_ctx else L.NL
    route, counts = _router(L, x, mod, g, w_router, with_ctx)
    cnt = counts[0, :N_EXPERTS].astype(jnp.int32)
    tiles_e = (cnt + tmx - 1) // tmx
    tile_end = jnp.cumsum(tiles_e)
    start = (tile_end - tiles_e) * tmx
    n_tiles = (2 * n_tok) // tmx + N_EXPERTS
    tile_id = jnp.arange(n_tiles, dtype=jnp.int32)
    tile_expert = jnp.minimum(jnp.sum(tile_id[:, None] >= tile_end[None, :], axis=1), N_EXPERTS - 1).astype(jnp.int32)
    tile_valid = (tile_id < tile_end[-1]).astype(jnp.int32)
    rt = route[:n_tok]
    sel = jnp.stack([rt[:, R_I1], rt[:, R_I2]], axis=1).astype(jnp.int32)
    rank = jnp.stack([rt[:, R_R1], rt[:, R_R2]], axis=1).astype(jnp.int32)
    pos = (jnp.sum(jnp.where(sel[..., None] == jnp.arange(N_EXPERTS), start, 0), axis=-1) + rank).reshape(-1)
    xs = _dispatch(L, x, mod, g, pos, n_tiles * tmx, with_ctx)
    ys = _experts(L, xs, tile_expert, tile_valid, we_gu, we_down, tmx)
    return _combine(L, ys, pos, route, x, mod, with_ctx)


def _rope_tables(L, lane_dim, lane_first, lane_freq, lane_is_col, half):
    t = jnp.arange(L.T, dtype=jnp.int32)
    rows = (t // GRID_W).astype(F32)
    cols = (t % GRID_W).astype(F32)
    inv_freq = jnp.exp(-math.log(ROPE_THETA) * jnp.arange(half, dtype=F32) / half)
    pos = jnp.where(lane_is_col[None, :], cols[:, None], rows[:, None])
    ang = pos * inv_freq[lane_freq][None, :]
    cos = jnp.where(lane_dim[None, :], jnp.cos(ang), 1.0)
    sin = jnp.where(lane_dim[None, :], jnp.sin(ang), 0.0)
    sin = jnp.where(lane_first[None, :], -sin, sin)
    ident = jnp.ones((L.tm, LANES), F32)
    return jnp.concatenate([cos, ident], axis=0), jnp.concatenate([sin, 0.0 * ident], axis=0)


def _gqa_rope_tables(L):
    lane = jnp.arange(LANES)
    d = lane % HEAD_DIM
    dd = d % (HEAD_DIM // 2)
    quarter = HEAD_DIM // 4
    return _rope_tables(L, lane >= 0, dd < quarter, dd % quarter, d >= HEAD_DIM // 2, quarter)


def _mla_rope_tables(L):
    lane = jnp.arange(LANES)
    d = lane - NOPE_DIM
    in_rope = (d >= 0) & (d < ROPE_DIM)
    dd = d % (ROPE_DIM // 2)
    quarter = ROPE_DIM // 4
    return _rope_tables(L, in_rope, dd < quarter, dd % quarter, d >= ROPE_DIM // 2, quarter)


def _segment_matrix(bounds):
    lane = jnp.arange(LANES)
    m = jnp.zeros((LANES, LANES), F32)
    for lo, hi in bounds:
        inside = (lane >= lo) & (lane < hi)
        m = m + jnp.where(inside[:, None] & inside[None, :], 1.0 / (hi - lo), 0.0)
    return m.astype(BF16)


def _natten_bias(rel_bias):
    n_h = rel_bias.shape[0]
    col = jnp.arange(GRID_W, dtype=jnp.int32)
    c0 = jnp.clip(col - WIN_C // 2, 0, GRID_W - WIN_C)
    col_ok = (col[None, :] >= c0[:, None]) & (col[None, :] < c0[:, None] + WIN_C)
    col_idx = jnp.clip(col[None, :] - col[:, None] + WIN_C - 1, 0, 2 * WIN_C - 2)
    full = jnp.where(col_ok[None, None], rel_bias[:, :, col_idx], NEG_INF) * LOG2E
    out = []
    for d in range(WIN_R):
        lo = WIN_R - 1 - d
        b = full[:, lo:lo + WIN_R]
        out.append(b.transpose(0, 2, 1, 3).reshape(n_h, GRID_W, WIN_R * GRID_W))
    return jnp.stack(out, axis=1).astype(F32)


def _dup_heads(w, n_heads):
    k = w.shape[0]
    w = w.reshape(k, n_heads, HEAD_DIM)
    return jnp.concatenate([w, w], axis=-1).reshape(k, n_heads * LANES)


def _half_pad_heads(w, n_heads, second):
    k = w.shape[0]
    w = w.reshape(k, n_heads, HEAD_DIM)
    z = jnp.zeros_like(w)
    return jnp.concatenate([z, w] if second else [w, z], axis=-1).reshape(k, n_heads * LANES)


def _ones_row(n_heads, second):
    lane = jnp.arange(LANES)
    return jnp.tile((lane == ONES_LANE[1 if second else 0]).astype(F32), n_heads)


def _alternate_heads(first, second, n_heads):
    lead = first.shape[:-1]
    f = first.reshape(lead + (n_heads, LANES))
    g = second.reshape(lead + (n_heads, LANES))
    odd = (jnp.arange(n_heads) % 2 == 1)[:, None]
    return jnp.where(odd, g, f).reshape(lead + (n_heads * LANES,))


def _logit_bound(q_sq, k_sq, scale):
    return (jnp.sqrt(q_sq * k_sq) * (scale * LOG2E * BOUND_MARGIN)).reshape(1, 1).astype(F32)


def _gain_sq(gain):
    return gain.shape[0] * jnp.max(jnp.abs(gain)) ** 2


def _gqa_mixer(L, x, mod, g_mix, p, need_ctx, tables):
    w = p["w_qkv"]
    nq = A_HEADS * HEAD_DIM
    nkv = A_KV_HEADS * HEAD_DIM
    w_v = w[:, nq + nkv:]
    w_p = jnp.concatenate([w[:, :nq], _dup_heads(w[:, nq:nq + nkv], A_KV_HEADS),
                           _half_pad_heads(w_v, A_KV_HEADS, False), _half_pad_heads(w_v, A_KV_HEADS, True)], axis=1)
    n_blk = A_KV_HEADS * LANES
    scale = HEAD_DIM ** -0.5
    gain = jnp.concatenate([jnp.tile(p["q_gain"], A_HEADS) * (scale * LOG2E), jnp.tile(p["k_gain"], 2 * A_KV_HEADS),
                            jnp.ones((2 * n_blk,), F32)]).reshape(1, -1)
    colbias = jnp.concatenate([jnp.zeros((nq + n_blk,), F32), _ones_row(A_KV_HEADS, False),
                               _ones_row(A_KV_HEADS, True)]).reshape(1, -1)
    seg = _segment_matrix([(0, HEAD_DIM), (HEAD_DIM, LANES)])
    tn = 512
    qkv = _proj(L, "gqa_qkv", x, w_p, tn=tn, mod=mod, g=g_mix, seg=seg, gain=gain, colbias=colbias,
                n_norm=(nq + n_blk) // tn, rope=(tables["gqa_cos"], tables["gqa_sin"], HEAD_DIM // 4))
    q_blocks = nq // LANES
    k_blocks = n_blk // LANES
    rep_pairs = N_PAIRS // A_KV_HEADS
    bound = _logit_bound(_gain_sq(p["q_gain"]), _gain_sq(p["k_gain"]), scale)
    return _attention(L, "gqa_attn", bound, qkv, [lambda p_: p_], qkv, [lambda p_: q_blocks + p_ // rep_pairs],
                      qkv, [lambda p_: q_blocks + k_blocks + p_ // rep_pairs,
                            lambda p_: q_blocks + 2 * k_blocks + p_ // rep_pairs],
                      pair=True, mode="all" if need_ctx else "lat")


def _natten_mixer(L, x, mod, g_mix, p, need_ctx):
    n = B_HEADS * HEAD_DIM
    w = p["w_qkv"]
    w_v = w[:, 2 * n:]
    w_p = jnp.concatenate([w[:, :2 * n], _alternate_heads(_half_pad_heads(w_v, B_HEADS, False),
                                                          _half_pad_heads(w_v, B_HEADS, True), B_HEADS)], axis=1)
    scale = HEAD_DIM ** -0.5
    n_v = B_HEADS * LANES
    gain = jnp.concatenate([jnp.tile(p["q_gain"], B_HEADS) * (scale * LOG2E), jnp.tile(p["k_gain"], B_HEADS),
                            jnp.ones((n_v,), F32)]).reshape(1, -1)
    colbias = jnp.concatenate([jnp.zeros((2 * n,), F32),
                               _alternate_heads(_ones_row(B_HEADS, False), _ones_row(B_HEADS, True), B_HEADS)]
                              ).reshape(1, -1)
    seg = _segment_matrix([(0, HEAD_DIM), (HEAD_DIM, LANES)])
    tn = 512
    qkv = _proj(L, "nat_qkv", x, w_p, tn=tn, mod=mod, g=g_mix, seg=seg, gain=gain, colbias=colbias,
                n_norm=2 * n // tn)
    qk_bound = _logit_bound(_gain_sq(p["q_gain"]), _gain_sq(p["k_gain"]), scale)
    bound = qk_bound + jnp.maximum(jnp.max(p["rel_bias"]), 0.0) * LOG2E
    o = _natten(L, bound, qkv, _natten_bias(p["rel_bias"]))
    if need_ctx:
        o_ctx = _attention(L, "nat_ctx_attn", qk_bound, qkv, [lambda p_: p_], qkv, [lambda p_: N_PAIRS + p_],
                           qkv, [lambda p_: 2 * N_PAIRS + 2 * p_, lambda p_: 2 * N_PAIRS + 2 * p_ + 1],
                           pair=True, mode="ctx")
        o = jnp.concatenate([o, o_ctx], axis=0)
    return o


def _mla_mixer(L, x, mod, g_mix, p, need_ctx, tables):
    d = L.D
    w_in = p["w_in"]
    n_c = Q_LORA + KV_LORA
    w_in_p = jnp.concatenate([w_in[:, :n_c], jnp.zeros((d, NOPE_DIM), F32), w_in[:, n_c:],
                              jnp.zeros((d, LANES - NOPE_DIM - ROPE_DIM), F32)], axis=1)
    dq = NOPE_DIM + ROPE_DIM
    pad_q = jnp.zeros((LANES - dq,), F32)
    seg_rope = _segment_matrix([(NOPE_DIM, dq)])
    gain_kr = jnp.concatenate([jnp.zeros((NOPE_DIM,), F32), p["k_gain"][NOPE_DIM:], pad_q]).reshape(1, LANES)
    cos, sin = tables["mla_cos"], tables["mla_sin"]
    cq, ckv, kr = _mla_in(L, x, mod, g_mix, w_in_p, p["g_dq"], p["g_dkv"], seg_rope, gain_kr, cos, sin)

    scale = dq ** -0.5
    w_uq = jnp.pad(p["w_uq"].reshape(Q_LORA, C_HEADS, dq), ((0, 0), (0, 0), (0, LANES - dq)))
    gain_q = jnp.tile(jnp.concatenate([p["q_gain"], pad_q]) * (scale * LOG2E), C_HEADS).reshape(1, -1)
    seg_q = _segment_matrix([(0, NOPE_DIM), (NOPE_DIM, dq)])
    n_hl = C_HEADS * LANES
    tn = 512
    q = _proj(L, "mla_q", cq, w_uq.reshape(Q_LORA, n_hl), tn=tn, seg=seg_q, gain=gain_q, n_norm=n_hl // tn,
              rope=(cos, sin, ROPE_DIM // 4))

    w_ukv = p["w_ukv"].reshape(KV_LORA, C_HEADS, NOPE_DIM + V_DIM)
    w_uk = jnp.pad(w_ukv[:, :, :NOPE_DIM], ((0, 0), (0, 0), (0, LANES - NOPE_DIM))).reshape(KV_LORA, n_hl)
    w_v = w_ukv[:, :, NOPE_DIM:].reshape(KV_LORA, C_HEADS * V_DIM)
    w_uv = _alternate_heads(_half_pad_heads(w_v, C_HEADS, False), _half_pad_heads(w_v, C_HEADS, True), C_HEADS)
    gain_k = jnp.tile(jnp.concatenate([p["k_gain"][:NOPE_DIM], jnp.zeros((LANES - NOPE_DIM,), F32)]),
                      C_HEADS).reshape(1, -1)
    seg_k = _segment_matrix([(0, NOPE_DIM)])
    k = _proj(L, "mla_k", ckv, w_uk, tn=tn, seg=seg_k, gain=gain_k, n_norm=n_hl // tn, add=kr)
    ones = _alternate_heads(_ones_row(C_HEADS, False), _ones_row(C_HEADS, True), C_HEADS).reshape(1, -1)
    v = _proj(L, "mla_v", ckv, w_uv, tn=tn, colbias=ones)
    q_sq = _gain_sq(p["q_gain"][:NOPE_DIM]) + _gain_sq(p["q_gain"][NOPE_DIM:])
    k_sq = _gain_sq(p["k_gain"][:NOPE_DIM]) + _gain_sq(p["k_gain"][NOPE_DIM:])
    bound = _logit_bound(q_sq, k_sq, scale)
    heads = [lambda p_: 2 * p_, lambda p_: 2 * p_ + 1]
    return _attention(L, "mla_attn", bound, q, heads, k, heads, v, heads, pair=False,
                      mode="all" if need_ctx else "lat")


def kernel(x, c, ctx, c_ctx, l0_w_mod, l0_b_mod, l0_g_mix, l0_g_ffn, l0_w_qkv, l0_q_gain, l0_k_gain, l0_w_o, l0_w_gu, l0_w_down, l1_w_mod, l1_b_mod, l1_g_mix, l1_g_ffn, l1_w_qkv, l1_q_gain, l1_k_gain, l1_rel_bias, l1_w_o, l1_w_router, l1_we_gu, l1_we_down, l2_w_mod, l2_b_mod, l2_g_mix, l2_g_ffn, l2_w_in, l2_g_dq, l2_g_dkv, l2_w_uq, l2_w_ukv, l2_q_gain, l2_k_gain, l2_w_o, l2_w_gu, l2_w_down, l3_w_mod, l3_b_mod, l3_g_mix, l3_g_ffn, l3_w_qkv, l3_q_gain, l3_k_gain, l3_w_o, l3_w_router, l3_we_gu, l3_we_down):
    layers = [
        dict(w_mod=l0_w_mod, b_mod=l0_b_mod, g_mix=l0_g_mix, g_ffn=l0_g_ffn, kind="gqa",
             mix=dict(w_qkv=l0_w_qkv, q_gain=l0_q_gain, k_gain=l0_k_gain), w_o=l0_w_o,
             ffn=dict(w_gu=l0_w_gu, w_down=l0_w_down)),
        dict(w_mod=l1_w_mod, b_mod=l1_b_mod, g_mix=l1_g_mix, g_ffn=l1_g_ffn, kind="natten",
             mix=dict(w_qkv=l1_w_qkv, q_gain=l1_q_gain, k_gain=l1_k_gain, rel_bias=l1_rel_bias), w_o=l1_w_o,
             moe=dict(w_router=l1_w_router, we_gu=l1_we_gu, we_down=l1_we_down)),
        dict(w_mod=l2_w_mod, b_mod=l2_b_mod, g_mix=l2_g_mix, g_ffn=l2_g_ffn, kind="mla",
             mix=dict(w_in=l2_w_in, g_dq=l2_g_dq, g_dkv=l2_g_dkv, w_uq=l2_w_uq, w_ukv=l2_w_ukv,
                      q_gain=l2_q_gain, k_gain=l2_k_gain), w_o=l2_w_o,
             ffn=dict(w_gu=l2_w_gu, w_down=l2_w_down)),
        dict(w_mod=l3_w_mod, b_mod=l3_b_mod, g_mix=l3_g_mix, g_ffn=l3_g_ffn, kind="gqa",
             mix=dict(w_qkv=l3_w_qkv, q_gain=l3_q_gain, k_gain=l3_k_gain), w_o=l3_w_o,
             moe=dict(w_router=l3_w_router, we_gu=l3_we_gu, we_down=l3_we_down)),
    ]
    batch, seq, d_model = x.shape
    L = _Layout(batch, seq, ctx.shape[1], d_model)
    gqa_cos, gqa_sin = _gqa_rope_tables(L)
    mla_cos, mla_sin = _mla_rope_tables(L)
    tables = dict(gqa_cos=gqa_cos, gqa_sin=gqa_sin, mla_cos=mla_cos, mla_sin=mla_sin)
    cvec = jnp.concatenate([c, c_ctx[None, :], jnp.zeros((SUBLANES - batch - 1, d_model), F32)], axis=0)
    xs = jnp.concatenate([x.reshape(L.NL, d_model), ctx.reshape(L.NC, d_model)], axis=0)
    for li, p in enumerate(layers):
        need_ctx = li < len(layers) - 1
        mod = _adaln(cvec, p["w_mod"], p["b_mod"], batch + 1)
        if p["kind"] == "gqa":
            o = _gqa_mixer(L, xs, mod, p["g_mix"], p["mix"], need_ctx, tables)
        elif p["kind"] == "natten":
            o = _natten_mixer(L, xs, mod, p["g_mix"], p["mix"], need_ctx)
        else:
            o = _mla_mixer(L, xs, mod, p["g_mix"], p["mix"], need_ctx, tables)
        xs = _oproj(L, o, p["w_o"], xs, mod, need_ctx)
        if "ffn" in p:
            xs = _ffn(L, xs, mod, p["g_ffn"], p["ffn"]["w_gu"], p["ffn"]["w_down"], need_ctx)
        else:
            xs = _moe(L, xs, mod, p["g_ffn"], p["moe"]["w_router"], p["moe"]["we_gu"], p["moe"]["we_down"], need_ctx)
    return xs[:L.NL].reshape(batch, seq, d_model)
```

```python
import functools
import math

import jax
import jax.numpy as jnp
from jax import lax
from jax.experimental import pallas as pl
from jax.experimental.pallas import tpu as pltpu

F32 = jnp.float32
BF16 = jnp.bfloat16

LANES = 128
SUBLANES = 8
VMEM_LIMIT_BYTES = 56 * 1024 * 1024

NORM_EPS = 1e-6
ROPE_THETA = 10000.0
NEG_INF = -1e30
GRID_W = 64
HEAD_DIM = 64
WIN_R = 8
WIN_C = 16
A_HEADS = 16
A_KV_HEADS = 4
B_HEADS = 16
C_HEADS = 16
Q_LORA = 384
KV_LORA = 256
NOPE_DIM = 64
ROPE_DIM = 32
V_DIM = 64
N_EXPERTS = 8
N_PAIRS = 8
LOG2E = math.log2(math.e)
FAST_BOUND = 40.0
BOUND_MARGIN = 1.02
ONES_LANE = (HEAD_DIM, 0)
NAT_ROWS = 4
NAT_SPAN = NAT_ROWS + WIN_R

SH1, SC1, G1, SH2, SC2, G2 = range(6)


def _params(sem):
    return pltpu.CompilerParams(dimension_semantics=sem, vmem_limit_bytes=VMEM_LIMIT_BYTES)


def _dot(a, b):
    return jnp.dot(a, b, preferred_element_type=F32)


def _dot_nt(a, b):
    return lax.dot_general(a, b, (((1,), (1,)), ((), ())), preferred_element_type=F32)


def _silu(x):
    return x * (1.0 / (1.0 + jnp.exp(-x)))


def _norm_mod(x, g, shift, scale):
    ms = jnp.mean(x * x, axis=-1, keepdims=True)
    y = x * lax.rsqrt(ms + NORM_EPS) * g
    return y * (1.0 + scale) + shift


def _seg_mean_sq(y, seg):
    return _dot((y * y).astype(BF16), seg)


def _rope(y, cos, sin_signed, shift):
    lane = lax.broadcasted_iota(jnp.int32, (1, y.shape[1]), 1)
    first = (lane & (2 * shift - 1)) < shift
    partner = jnp.where(first, pltpu.roll(y, LANES - shift, 1), pltpu.roll(y, shift, 1))
    return y * cos + partner * sin_signed


class _Layout:
    def __init__(self, batch, seq, ctx_len, d_model):
        self.B, self.T, self.TC, self.D = batch, seq, ctx_len, d_model
        self.NL = batch * seq
        self.NC = batch * ctx_len
        self.N = self.NL + self.NC
        self.tm = 1024 if (self.NC % 1024 == 0 and seq % 1024 == 0) else ctx_len
        assert seq % self.tm == 0 and self.NC % self.tm == 0
        self.tpb = seq // self.tm
        self.tq = ctx_len
        assert seq % self.tq == 0 and seq % GRID_W == 0

    def mod_group(self, i):
        return jnp.minimum(i // self.tpb, self.B)

    def rope_block(self, i):
        return jnp.where(i < self.B * self.tpb, i % self.tpb, self.tpb)

    def rows(self, with_ctx):
        return self.N if with_ctx else self.NL

    def tiles(self, with_ctx):
        return self.rows(with_ctx) // self.tm


def _adaln_kernel(c_ref, w_ref, b_ref, o_ref):
    s = _silu(c_ref[...])
    o_ref[...] = jnp.dot(s, w_ref[...], preferred_element_type=F32,
                         precision=lax.Precision.HIGHEST) + b_ref[...]


def _adaln(cvec, w_mod, b_mod, n_groups):
    d, n_out = w_mod.shape
    tn = n_out // 4
    out = pl.pallas_call(
        _adaln_kernel,
        out_shape=jax.ShapeDtypeStruct((SUBLANES, n_out), F32),
        grid=(n_out // tn,),
        in_specs=[pl.BlockSpec((SUBLANES, d), lambda j: (0, 0)),
                  pl.BlockSpec((d, tn), lambda j: (0, j)),
                  pl.BlockSpec((1, tn), lambda j: (0, j))],
        out_specs=pl.BlockSpec((SUBLANES, tn), lambda j: (0, j)),
        compiler_params=_params(("arbitrary",)),
        name="adaln",
    )(cvec, w_mod, b_mod.reshape(1, n_out))
    mod = out[:n_groups].reshape(n_groups, 6, d)
    return jnp.pad(mod, ((0, 0), (0, 2), (0, 0)))


def _proj_kernel(*refs, has_mod, n_norm, n_col_tiles, rope_shift, has_add, has_colbias, groups):
    refs = list(refs)
    x_ref = refs.pop(0)
    if has_mod:
        mod_ref, g_ref = refs.pop(0), refs.pop(0)
    w_ref = refs.pop(0)
    if n_norm:
        seg_ref, gain_ref = refs.pop(0), refs.pop(0)
    if rope_shift:
        cos_ref, sin_ref = refs.pop(0), refs.pop(0)
    if has_add:
        add_ref = refs.pop(0)
    if has_colbias:
        colbias_ref = refs.pop(0)
    o_ref = refs.pop(0)
    j = pl.program_id(1)

    if has_mod:
        h_scr = refs.pop(0)

        @pl.when(j == 0)
        def _():
            h = _norm_mod(x_ref[...], g_ref[...], mod_ref[SH1:SH1 + 1, :], mod_ref[SC1:SC1 + 1, :])
            h_scr[...] = h.astype(BF16)

        h = h_scr[...]
    else:
        h = x_ref[...]
    acc = _dot(h, w_ref[...].astype(BF16))

    def normed():
        for c in range(groups):
            sl = slice(c * LANES, (c + 1) * LANES)
            y = acc[:, sl]
            y = y * lax.rsqrt(_seg_mean_sq(y, seg_ref[...]) + NORM_EPS) * gain_ref[:, sl]
            if rope_shift:
                y = _rope(y, cos_ref[...], sin_ref[...], rope_shift)
            if has_add:
                y = y + add_ref[...]
            o_ref[:, sl] = y.astype(o_ref.dtype)

    def plain():
        y = acc + colbias_ref[...] if has_colbias else acc
        o_ref[...] = y.astype(o_ref.dtype)

    if n_norm == 0:
        plain()
    elif n_norm == n_col_tiles:
        normed()
    else:
        pl.when(j < n_norm)(normed)
        pl.when(j >= n_norm)(plain)


def _proj(L, name, x, w, *, with_ctx=True, tn=512, mod=None, g=None, seg=None, gain=None, n_norm=0,
          rope=None, add=None, colbias=None, out_dtype=BF16):
    k, n_out = w.shape
    tm = L.tm
    n_col = n_out // tn
    in_specs = [pl.BlockSpec((tm, k), lambda i, j: (i, 0))]
    args = [x]
    scratch = []
    if mod is not None:
        in_specs += [pl.BlockSpec((None, SUBLANES, k), lambda i, j: (L.mod_group(i), 0, 0)),
                     pl.BlockSpec((1, k), lambda i, j: (0, 0))]
        args += [mod, g.reshape(1, k)]
        scratch.append(pltpu.VMEM((tm, k), BF16))
    in_specs.append(pl.BlockSpec((k, tn), lambda i, j: (0, j)))
    args.append(w)
    if n_norm:
        in_specs += [pl.BlockSpec((LANES, LANES), lambda i, j: (0, 0)),
                     pl.BlockSpec((1, tn), lambda i, j: (0, j))]
        args += [seg, gain]
    rope_shift = 0
    if rope is not None:
        cos, sin, rope_shift = rope
        in_specs += [pl.BlockSpec((tm, LANES), lambda i, j: (L.rope_block(i), 0))] * 2
        args += [cos, sin]
    if add is not None:
        in_specs.append(pl.BlockSpec((tm, LANES), lambda i, j: (i, 0)))
        args.append(add)
    if colbias is not None:
        in_specs.append(pl.BlockSpec((1, tn), lambda i, j: (0, j)))
        args.append(colbias)
    kern = functools.partial(_proj_kernel, has_mod=mod is not None, n_norm=n_norm, n_col_tiles=n_col,
                             rope_shift=rope_shift, has_add=add is not None, has_colbias=colbias is not None,
                             groups=tn // LANES)
    return pl.pallas_call(
        kern,
        out_shape=jax.ShapeDtypeStruct((L.N, n_out), out_dtype),
        grid=(L.tiles(with_ctx), n_col),
        in_specs=in_specs,
        out_specs=pl.BlockSpec((tm, tn), lambda i, j: (i, j)),
        scratch_shapes=scratch,
        compiler_params=_params(("parallel", "arbitrary")),
        name=name,
    )(*args)


def _mla_in_kernel(x_ref, mod_ref, g_ref, w_ref, gdq_ref, gdkv_ref, seg_ref, gain_ref, cos_ref, sin_ref,
                   cq_ref, ckv_ref, kr_ref):
    h = _norm_mod(x_ref[...], g_ref[...], mod_ref[SH1:SH1 + 1, :], mod_ref[SC1:SC1 + 1, :])
    acc = _dot(h.astype(BF16), w_ref[...].astype(BF16))
    cq = acc[:, :Q_LORA]
    cq = cq * lax.rsqrt(jnp.mean(cq * cq, axis=-1, keepdims=True) + NORM_EPS) * gdq_ref[...]
    cq_ref[...] = cq.astype(BF16)
    ckv = acc[:, Q_LORA:Q_LORA + KV_LORA]
    ckv = ckv * lax.rsqrt(jnp.mean(ckv * ckv, axis=-1, keepdims=True) + NORM_EPS) * gdkv_ref[...]
    ckv_ref[...] = ckv.astype(BF16)
    kr = acc[:, Q_LORA + KV_LORA:]
    kr = kr * lax.rsqrt(_seg_mean_sq(kr, seg_ref[...]) + NORM_EPS) * gain_ref[...]
    kr_ref[...] = _rope(kr, cos_ref[...], sin_ref[...], ROPE_DIM // 4)


def _mla_in(L, x, mod, g, w_in_p, g_dq, g_dkv, seg_kr, gain_kr, cos, sin):
    tm, d = L.tm, L.D
    n_out = w_in_p.shape[1]
    full = lambda shape: pl.BlockSpec(shape, lambda i: (0,) * len(shape))
    return pl.pallas_call(
        _mla_in_kernel,
        out_shape=(jax.ShapeDtypeStruct((L.N, Q_LORA), BF16),
                   jax.ShapeDtypeStruct((L.N, KV_LORA), BF16),
                   jax.ShapeDtypeStruct((L.N, LANES), F32)),
        grid=(L.tiles(True),),
        in_specs=[pl.BlockSpec((tm, d), lambda i: (i, 0)),
                  pl.BlockSpec((None, SUBLANES, d), lambda i: (L.mod_group(i), 0, 0)),
                  full((1, d)), full((d, n_out)), full((1, Q_LORA)), full((1, KV_LORA)),
                  full((LANES, LANES)), full((1, LANES)),
                  pl.BlockSpec((tm, LANES), lambda i: (L.rope_block(i), 0)),
                  pl.BlockSpec((tm, LANES), lambda i: (L.rope_block(i), 0))],
        out_specs=(pl.BlockSpec((tm, Q_LORA), lambda i: (i, 0)),
                   pl.BlockSpec((tm, KV_LORA), lambda i: (i, 0)),
                   pl.BlockSpec((tm, LANES), lambda i: (i, 0))),
        compiler_params=_params(("parallel",)),
        name="mla_in",
    )(x, mod, g.reshape(1, d), w_in_p, g_dq.reshape(1, -1), g_dkv.reshape(1, -1), seg_kr, gain_kr, cos, sin)


def _softmax_pv(s_list, v_list, bound, ones_lane, fast):
    if fast:
        o = None
        for s, v in zip(s_list, v_list):
            part = _dot(jnp.exp2(s - bound).astype(BF16), v)
            o = part if o is None else o + part
        return o / o[:, ones_lane:ones_lane + 1]
    m = None
    for s in s_list:
        ms = jnp.max(s, axis=-1, keepdims=True)
        m = ms if m is None else jnp.maximum(m, ms)
    o, den = None, None
    for s, v in zip(s_list, v_list):
        p = jnp.exp2(s - m)
        ds = jnp.sum(p, axis=-1, keepdims=True)
        part = _dot(p.astype(BF16), v)
        o = part if o is None else o + part
        den = ds if den is None else den + ds
    return o / den


def _attn_kernel(*refs, nq, nk, pair, lat_keys, nq_lat, ctx_q):
    refs = list(refs)
    bound_ref = refs.pop(0)
    q_refs = [refs.pop(0) for _ in range(nq)]
    kl_refs = [refs.pop(0) for _ in range(nk)] if lat_keys else []
    kc_refs = [refs.pop(0) for _ in range(nk)]
    vl_refs = [refs.pop(0) for _ in range(2)] if lat_keys else []
    vc_refs = [refs.pop(0) for _ in range(2)]
    o_ref = refs.pop(0)
    lo = lax.broadcasted_iota(jnp.int32, (1, LANES), 1) < HEAD_DIM
    bound = bound_ref[0, 0]

    def run(use_lat, fast):
        outs = []
        for hh in range(2):
            q = q_refs[hh % nq][...]
            if pair:
                q = jnp.where(lo if hh == 0 else jnp.logical_not(lo), q, jnp.zeros_like(q))
            s_list = [_dot_nt(q, kc_refs[hh % nk][...])]
            v_list = [vc_refs[hh][...]]
            if use_lat:
                s_list.append(_dot_nt(q, kl_refs[hh % nk][...]))
                v_list.append(vl_refs[hh][...])
            outs.append(_softmax_pv(s_list, v_list, bound, ONES_LANE[hh], fast))
        o_ref[...] = jnp.where(lo, outs[0], outs[1]).astype(o_ref.dtype)

    is_fast = bound <= FAST_BOUND
    for fast in (True, False):
        pred = is_fast if fast else jnp.logical_not(is_fast)
        if lat_keys and ctx_q:
            i = pl.program_id(2)
            pl.when(jnp.logical_and(pred, i < nq_lat))(functools.partial(run, True, fast))
            pl.when(jnp.logical_and(pred, i == nq_lat))(functools.partial(run, False, fast))
        else:
            pl.when(pred)(functools.partial(run, lat_keys, fast))


def _attention(L, name, bound, q, q_cols, k, k_cols, v, v_cols, *, pair, mode):
    B, T, TC, tq = L.B, L.T, L.TC, L.tq
    nq_lat = T // tq
    ctx_row = L.NL // TC
    lat_keys = mode != "ctx"
    ctx_q = mode != "lat"
    n_i = {"all": nq_lat + 1, "lat": nq_lat, "ctx": 1}[mode]

    def q_row(b, i):
        if mode == "ctx":
            return ctx_row + b
        if mode == "lat":
            return b * nq_lat + i
        return jnp.where(i < nq_lat, b * nq_lat + i, ctx_row + b)

    in_specs = [pl.BlockSpec(memory_space=pltpu.SMEM)]
    args = [bound]
    for f in q_cols:
        in_specs.append(pl.BlockSpec((tq, LANES), lambda b, p, i, f=f: (q_row(b, i), f(p))))
        args.append(q)
    if lat_keys:
        for f in k_cols:
            in_specs.append(pl.BlockSpec((T, LANES), lambda b, p, i, f=f: (b, f(p))))
            args.append(k)
    for f in k_cols:
        in_specs.append(pl.BlockSpec((TC, LANES), lambda b, p, i, f=f: (ctx_row + b, f(p))))
        args.append(k)
    if lat_keys:
        for f in v_cols:
            in_specs.append(pl.BlockSpec((T, LANES), lambda b, p, i, f=f: (b, f(p))))
            args.append(v)
    for f in v_cols:
        in_specs.append(pl.BlockSpec((TC, LANES), lambda b, p, i, f=f: (ctx_row + b, f(p))))
        args.append(v)
    out_rows = {"all": L.N, "lat": L.NL, "ctx": L.NC}[mode]
    out_row = (lambda b, i: b) if mode == "ctx" else q_row
    kern = functools.partial(_attn_kernel, nq=len(q_cols), nk=len(k_cols), pair=pair, lat_keys=lat_keys,
                             nq_lat=nq_lat, ctx_q=ctx_q)
    return pl.pallas_call(
        kern,
        out_shape=jax.ShapeDtypeStruct((out_rows, N_PAIRS * LANES), BF16),
        grid=(B, N_PAIRS, n_i),
        in_specs=in_specs,
        out_specs=pl.BlockSpec((tq, LANES), lambda b, p, i: (out_row(b, i), p)),
        compiler_params=_params(("parallel", "parallel", "arbitrary")),
        name=name,
    )(*args)


def _natten_kernel(bound_ref, q_ref, k_ref, va_ref, vb_ref, kc_ref, vca_ref, vcb_ref, bias_ref, o_ref, oc_scr, *,
                   rows_n):
    lo = lax.broadcasted_iota(jnp.int32, (1, LANES), 1) < HEAD_DIM
    halves = (lo, jnp.logical_not(lo))
    v_refs = (va_ref, vb_ref)
    vc_refs = (vca_ref, vcb_ref)
    n_groups = rows_n // NAT_ROWS
    bound = bound_ref[0, 0]

    def group(g):
        u0 = jnp.clip(g * NAT_ROWS - WIN_R // 2, 0, rows_n - NAT_SPAN)
        kind = jnp.where(g == 0, 0, jnp.where(g == n_groups - 1, 2, 1))
        rows = pl.ds(pl.multiple_of(g * (NAT_ROWS * GRID_W), NAT_ROWS * GRID_W), NAT_ROWS * GRID_W)
        keys = pl.ds(pl.multiple_of(u0 * GRID_W, GRID_W), NAT_SPAN * GRID_W)
        return kind, rows, keys

    def masked(q, hh):
        return jnp.where(halves[hh], q, jnp.zeros_like(q))

    def fast_body(g, carry):
        kind, rows, keys = group(g)
        q = q_ref[rows, :]
        kw = k_ref[keys, :]
        outs = []
        for hh in range(2):
            s_l = _dot_nt(masked(q, hh), kw) + bias_ref[hh, kind]
            o = oc_scr[hh, rows, :] + _dot(jnp.exp2(s_l - bound).astype(BF16), v_refs[hh][keys, :])
            outs.append(o / o[:, ONES_LANE[hh]:ONES_LANE[hh] + 1])
        o_ref[rows, :] = jnp.where(lo, outs[0], outs[1]).astype(o_ref.dtype)
        return carry

    def exact_body(g, carry):
        kind, rows, keys = group(g)
        q = q_ref[rows, :]
        kw = k_ref[keys, :]
        outs = []
        for hh in range(2):
            qh = masked(q, hh)
            s_l = _dot_nt(qh, kw) + bias_ref[hh, kind]
            s_c = _dot_nt(qh, kc_ref[...])
            outs.append(_softmax_pv([s_l, s_c], [v_refs[hh][keys, :], vc_refs[hh][...]], bound, ONES_LANE[hh], False))
        o_ref[rows, :] = jnp.where(lo, outs[0], outs[1]).astype(o_ref.dtype)
        return carry

    is_fast = bound <= FAST_BOUND

    @pl.when(is_fast)
    def _():
        q_all = q_ref[...]
        for hh in range(2):
            p_c = jnp.exp2(_dot_nt(masked(q_all, hh), kc_ref[...]) - bound).astype(BF16)
            oc_scr[hh] = _dot(p_c, vc_refs[hh][...])
        lax.fori_loop(0, n_groups, fast_body, 0, unroll=2)

    @pl.when(jnp.logical_not(is_fast))
    def _():
        lax.fori_loop(0, n_groups, exact_body, 0)


def _natten(L, bound, qkv, bias):
    B, T, TC = L.B, L.T, L.TC
    ctx_row = L.NL // TC
    rows_n = T // GRID_W
    assert rows_n % NAT_ROWS == 0 and rows_n >= NAT_SPAN and NAT_ROWS == WIN_R // 2
    k0, v0 = N_PAIRS, 2 * N_PAIRS
    return pl.pallas_call(
        functools.partial(_natten_kernel, rows_n=T // GRID_W),
        out_shape=jax.ShapeDtypeStruct((L.NL, N_PAIRS * LANES), BF16),
        grid=(B, N_PAIRS),
        in_specs=[pl.BlockSpec(memory_space=pltpu.SMEM),
                  pl.BlockSpec((T, LANES), lambda b, p: (b, p)),
                  pl.BlockSpec((T, LANES), lambda b, p: (b, k0 + p)),
                  pl.BlockSpec((T, LANES), lambda b, p: (b, v0 + 2 * p)),
                  pl.BlockSpec((T, LANES), lambda b, p: (b, v0 + 2 * p + 1)),
                  pl.BlockSpec((TC, LANES), lambda b, p: (ctx_row + b, k0 + p)),
                  pl.BlockSpec((TC, LANES), lambda b, p: (ctx_row + b, v0 + 2 * p)),
                  pl.BlockSpec((TC, LANES), lambda b, p: (ctx_row + b, v0 + 2 * p + 1)),
                  pl.BlockSpec((2, 3, NAT_ROWS * GRID_W, NAT_SPAN * GRID_W), lambda b, p: (p, 0, 0, 0))],
        out_specs=pl.BlockSpec((T, LANES), lambda b, p: (b, p)),
        scratch_shapes=[pltpu.VMEM((2, T, LANES), F32)],
        compiler_params=_params(("parallel", "parallel")),
        name="natten",
    )(bound, qkv, qkv, qkv, qkv, qkv, qkv, qkv, bias)


def _oproj_kernel(o_ref, w_ref, x_ref, mod_ref, out_ref):
    acc = _dot(o_ref[...], w_ref[...].astype(BF16))
    out_ref[...] = x_ref[...] + mod_ref[G1:G1 + 1, :] * acc


def _oproj(L, o, w_o, x, mod, with_ctx):
    tm, d = L.tm // 2, L.D
    k = w_o.shape[0]
    ratio = L.tm // tm
    return pl.pallas_call(
        _oproj_kernel,
        out_shape=jax.ShapeDtypeStruct((L.rows(with_ctx), d), F32),
        grid=(L.tiles(with_ctx) * ratio,),
        in_specs=[pl.BlockSpec((tm, k), lambda i: (i, 0)),
                  pl.BlockSpec((k, d), lambda i: (0, 0)),
                  pl.BlockSpec((tm, d), lambda i: (i, 0)),
                  pl.BlockSpec((None, SUBLANES, d), lambda i: (L.mod_group(i // ratio), 0, 0))],
        out_specs=pl.BlockSpec((tm, d), lambda i: (i, 0)),
        compiler_params=_params(("parallel",)),
        name="oproj",
    )(o, w_o, x, mod)


def _ffn_kernel(x_ref, mod_ref, g_ref, wg_ref, wu_ref, wd_ref, out_ref, h_scr, acc_scr):
    c = pl.program_id(1)

    @pl.when(c == 0)
    def _():
        h = _norm_mod(x_ref[...], g_ref[...], mod_ref[SH2:SH2 + 1, :], mod_ref[SC2:SC2 + 1, :])
        h_scr[...] = h.astype(BF16)
        acc_scr[...] = jnp.zeros_like(acc_scr)

    h = h_scr[...]
    gate = _dot(h, wg_ref[...].astype(BF16))
    up = _dot(h, wu_ref[...].astype(BF16))
    act = (_silu(gate) * up).astype(BF16)
    acc_scr[...] += _dot(act, wd_ref[...].astype(BF16))

    @pl.when(c == pl.num_programs(1) - 1)
    def _():
        out_ref[...] = x_ref[...] + mod_ref[G2:G2 + 1, :] * acc_scr[...]


def _ffn(L, x, mod, g, w_gu, w_down, with_ctx):
    tm, d = L.tm, L.D
    d_ff = w_down.shape[0]
    tf = 256
    n_ch = d_ff // tf
    return pl.pallas_call(
        _ffn_kernel,
        out_shape=jax.ShapeDtypeStruct((L.rows(with_ctx), d), F32),
        grid=(L.tiles(with_ctx), n_ch),
        in_specs=[pl.BlockSpec((tm, d), lambda i, c: (i, 0)),
                  pl.BlockSpec((None, SUBLANES, d), lambda i, c: (L.mod_group(i), 0, 0)),
                  pl.BlockSpec((1, d), lambda i, c: (0, 0)),
                  pl.BlockSpec((d, tf), lambda i, c: (0, c)),
                  pl.BlockSpec((d, tf), lambda i, c: (0, n_ch + c)),
                  pl.BlockSpec((tf, d), lambda i, c: (c, 0))],
        out_specs=pl.BlockSpec((tm, d), lambda i, c: (i, 0)),
        scratch_shapes=[pltpu.VMEM((tm, d), BF16), pltpu.VMEM((tm, d), F32)],
        compiler_params=_params(("parallel", "arbitrary")),
        name="ffn",
    )(x, mod, g.reshape(1, d), w_gu, w_gu, w_down)


R_I1, R_I2, R_W1, R_W2, R_R1, R_R2 = range(6)


def _router_kernel(x_ref, mod_ref, g_ref, wr_ref, route_ref, cnt_ref, carry_scr, *, tm):
    i = pl.program_id(0)

    @pl.when(i == 0)
    def _():
        carry_scr[...] = jnp.zeros_like(carry_scr)

    h = _norm_mod(x_ref[...], g_ref[...], mod_ref[SH2:SH2 + 1, :], mod_ref[SC2:SC2 + 1, :])
    logits = jnp.dot(h, wr_ref[...], preferred_element_type=F32, precision=lax.Precision.HIGHEST)
    lane = lax.broadcasted_iota(jnp.int32, logits.shape, 1).astype(F32)
    lg = jnp.where(lane < N_EXPERTS, logits, -jnp.inf)
    m1 = jnp.max(lg, axis=-1, keepdims=True)
    i1 = jnp.min(jnp.where(lg == m1, lane, float(LANES)), axis=-1, keepdims=True)
    lg2 = jnp.where(lane == i1, -jnp.inf, lg)
    m2 = jnp.max(lg2, axis=-1, keepdims=True)
    i2 = jnp.min(jnp.where(lg2 == m2, lane, float(LANES)), axis=-1, keepdims=True)
    e = jnp.exp(m2 - m1)
    w1 = 1.0 / (1.0 + e)
    w2 = e / (1.0 + e)
    onehot = jnp.where((lane == i1) | (lane == i2), 1.0, 0.0)
    row = lax.broadcasted_iota(jnp.int32, (tm, tm), 0)
    col = lax.broadcasted_iota(jnp.int32, (tm, tm), 1)
    tri = jnp.where(row > col, 1.0, 0.0).astype(BF16)
    rank = _dot(tri, onehot.astype(BF16)) + carry_scr[0:1, :]
    r1 = jnp.sum(jnp.where(lane == i1, rank, 0.0), axis=-1, keepdims=True)
    r2 = jnp.sum(jnp.where(lane == i2, rank, 0.0), axis=-1, keepdims=True)
    carry_scr[...] = carry_scr[...] + jnp.sum(onehot, axis=0, keepdims=True)
    rec = jnp.zeros_like(logits)
    for pos, val in ((R_I1, i1), (R_I2, i2), (R_W1, w1), (R_W2, w2), (R_R1, r1), (R_R2, r2)):
        rec = jnp.where(lane == float(pos), val, rec)
    route_ref[...] = rec
    cnt_ref[...] = carry_scr[...]


def _router(L, x, mod, g, w_router, with_ctx):
    tm, d = L.tm, L.D
    wr = jnp.pad(w_router, ((0, 0), (0, LANES - w_router.shape[1])))
    return pl.pallas_call(
        functools.partial(_router_kernel, tm=tm),
        out_shape=(jax.ShapeDtypeStruct((L.rows(with_ctx), LANES), F32),
                   jax.ShapeDtypeStruct((SUBLANES, LANES), F32)),
        grid=(L.tiles(with_ctx),),
        in_specs=[pl.BlockSpec((tm, d), lambda i: (i, 0)),
                  pl.BlockSpec((None, SUBLANES, d), lambda i: (L.mod_group(i), 0, 0)),
                  pl.BlockSpec((1, d), lambda i: (0, 0)),
                  pl.BlockSpec((d, LANES), lambda i: (0, 0))],
        out_specs=(pl.BlockSpec((tm, LANES), lambda i: (i, 0)),
                   pl.BlockSpec((SUBLANES, LANES), lambda i: (0, 0))),
        scratch_shapes=[pltpu.VMEM((SUBLANES, LANES), F32)],
        compiler_params=_params(("arbitrary",)),
        name="router",
    )(x, mod, g.reshape(1, d), wr)


def _row_block(ref, row):
    return ref.at[pl.ds(pl.multiple_of(row * SUBLANES, SUBLANES), SUBLANES)]


def _dispatch_kernel(pos_ref, x_ref, mod_ref, g_ref, xs_in_ref, xs_ref, rows_scr, sem, *, tmd):
    del xs_in_ref
    h = _norm_mod(x_ref[...], g_ref[...], mod_ref[SH2:SH2 + 1, :], mod_ref[SC2:SC2 + 1, :])
    for c in range(h.shape[1] // LANES):
        rows_scr[pl.ds(c, tmd, stride=SUBLANES), :] = h[:, c * LANES:(c + 1) * LANES]

    def copies(t):
        src = _row_block(rows_scr, t)
        return (pltpu.make_async_copy(src, _row_block(xs_ref, pos_ref[0, 2 * t]), sem),
                pltpu.make_async_copy(src, _row_block(xs_ref, pos_ref[0, 2 * t + 1]), sem))

    def start(t, carry):
        for k, cp in enumerate(copies(t)):
            cp.start(priority=k)
        return carry

    def wait(t, carry):
        for cp in copies(t):
            cp.wait()
        return carry

    lax.fori_loop(0, tmd, start, 0)
    lax.fori_loop(0, tmd, wait, 0)


def _dispatch(L, x, mod, g, pos, n_slots, with_ctx):
    tmd, d = L.tm, L.D
    n_t = L.tiles(with_ctx)
    xs0 = jnp.zeros((n_slots * SUBLANES, LANES), F32)
    return pl.pallas_call(
        functools.partial(_dispatch_kernel, tmd=tmd),
        out_shape=jax.ShapeDtypeStruct(xs0.shape, F32),
        grid=(n_t,),
        in_specs=[pl.BlockSpec((None, 1, 2 * tmd), lambda i: (i, 0, 0), memory_space=pltpu.SMEM),
                  pl.BlockSpec((tmd, d), lambda i: (i, 0)),
                  pl.BlockSpec((None, SUBLANES, d), lambda i: (L.mod_group(i), 0, 0)),
                  pl.BlockSpec((1, d), lambda i: (0, 0)),
                  pl.BlockSpec(memory_space=pl.ANY)],
        out_specs=pl.BlockSpec(memory_space=pl.ANY),
        scratch_shapes=[pltpu.VMEM((tmd * SUBLANES, LANES), F32), pltpu.SemaphoreType.DMA(())],
        input_output_aliases={4: 0},
        compiler_params=pltpu.CompilerParams(dimension_semantics=("arbitrary",), has_side_effects=True,
                                             vmem_limit_bytes=VMEM_LIMIT_BYTES),
        name="moe_dispatch",
    )(pos.reshape(-1, 1, 2 * tmd)[:n_t], x, mod, g.reshape(1, d), xs0)


def _experts_kernel(te_ref, tv_ref, xs_ref, wg_ref, wu_ref, wd_ref, ys_ref, h_scr, acc_scr, *, tmx):
    del te_ref
    t, c = pl.program_id(0), pl.program_id(1)
    valid = tv_ref[t] > 0
    groups = h_scr.shape[1] // LANES

    @pl.when(jnp.logical_and(valid, c == 0))
    def _():
        for k in range(groups):
            h_scr[:, k * LANES:(k + 1) * LANES] = xs_ref[pl.ds(k, tmx, stride=SUBLANES), :].astype(BF16)
        acc_scr[...] = jnp.zeros_like(acc_scr)

    @pl.when(valid)
    def _():
        h = h_scr[...]
        gate = _dot(h, wg_ref[...].astype(BF16))
        up = _dot(h, wu_ref[...].astype(BF16))
        act = (_silu(gate) * up).astype(BF16)
        acc_scr[...] += _dot(act, wd_ref[...].astype(BF16))

    @pl.when(jnp.logical_and(valid, c == pl.num_programs(1) - 1))
    def _():
        for k in range(groups):
            ys_ref[pl.ds(k, tmx, stride=SUBLANES), :] = acc_scr[:, k * LANES:(k + 1) * LANES]

    @pl.when(jnp.logical_and(jnp.logical_not(valid), c == 0))
    def _():
        ys_ref[...] = jnp.zeros_like(ys_ref)


def _experts(L, xs, tile_expert, tile_valid, we_gu, we_down, tmx):
    d = L.D
    d_ff = we_down.shape[1]
    tf = 512
    n_ch = d_ff // tf
    n_tiles = xs.shape[0] // (tmx * SUBLANES)

    def chunk(c, tv, t):
        return jnp.where(tv[t] > 0, c, n_ch - 1)

    grid_spec = pltpu.PrefetchScalarGridSpec(
        num_scalar_prefetch=2,
        grid=(n_tiles, n_ch),
        in_specs=[pl.BlockSpec((tmx * SUBLANES, LANES), lambda t, c, te, tv: (t, 0)),
                  pl.BlockSpec((None, d, tf), lambda t, c, te, tv: (te[t], 0, chunk(c, tv, t))),
                  pl.BlockSpec((None, d, tf), lambda t, c, te, tv: (te[t], 0, n_ch + chunk(c, tv, t))),
                  pl.BlockSpec((None, tf, d), lambda t, c, te, tv: (te[t], chunk(c, tv, t), 0))],
        out_specs=pl.BlockSpec((tmx * SUBLANES, LANES), lambda t, c, te, tv: (t, 0)),
        scratch_shapes=[pltpu.VMEM((tmx, d), BF16), pltpu.VMEM((tmx, d), F32)],
    )
    return pl.pallas_call(
        functools.partial(_experts_kernel, tmx=tmx),
        out_shape=jax.ShapeDtypeStruct(xs.shape, F32),
        grid_spec=grid_spec,
        compiler_params=_params(("arbitrary", "arbitrary")),
        name="moe_experts",
    )(tile_expert, tile_valid, xs, we_gu, we_gu, we_down)


def _combine_kernel(pos_ref, ys_ref, route_ref, x_ref, mod_ref, out_ref, buf1, buf2, sem, *, tmc):
    def copies(t):
        dst1 = _row_block(buf1, t)
        dst2 = _row_block(buf2, t)
        return (pltpu.make_async_copy(_row_block(ys_ref, pos_ref[0, 2 * t]), dst1, sem),
                pltpu.make_async_copy(_row_block(ys_ref, pos_ref[0, 2 * t + 1]), dst2, sem))

    def start(t, carry):
        for k, cp in enumerate(copies(t)):
            cp.start(priority=k)
        return carry

    def wait(t, carry):
        for cp in copies(t):
            cp.wait()
        return carry

    lax.fori_loop(0, tmc, start, 0)
    lax.fori_loop(0, tmc, wait, 0)
    w1 = route_ref[:, R_W1:R_W1 + 1]
    w2 = route_ref[:, R_W2:R_W2 + 1]
    for k in range(out_ref.shape[1] // LANES):
        sl = slice(k * LANES, (k + 1) * LANES)
        y = w1 * buf1[pl.ds(k, tmc, stride=SUBLANES), :] + w2 * buf2[pl.ds(k, tmc, stride=SUBLANES), :]
        out_ref[:, sl] = x_ref[:, sl] + mod_ref[G2:G2 + 1, sl] * y


def _combine(L, ys, pos, route, x, mod, with_ctx):
    tmc, d = L.TC, L.D
    ratio = L.tm // tmc
    n_t = L.tiles(with_ctx) * ratio
    return pl.pallas_call(
        functools.partial(_combine_kernel, tmc=tmc),
        out_shape=jax.ShapeDtypeStruct((L.rows(with_ctx), d), F32),
        grid=(n_t,),
        in_specs=[pl.BlockSpec((None, 1, 2 * tmc), lambda i: (i, 0, 0), memory_space=pltpu.SMEM),
                  pl.BlockSpec(memory_space=pl.ANY),
                  pl.BlockSpec((tmc, LANES), lambda i: (i, 0)),
                  pl.BlockSpec((tmc, d), lambda i: (i, 0)),
                  pl.BlockSpec((None, SUBLANES, d), lambda i: (L.mod_group(i // ratio), 0, 0))],
        out_specs=pl.BlockSpec((tmc, d), lambda i: (i, 0)),
        scratch_shapes=[pltpu.VMEM((tmc * SUBLANES, LANES), F32), pltpu.VMEM((tmc * SUBLANES, LANES), F32),
                        pltpu.SemaphoreType.DMA(())],
        compiler_params=_params(("arbitrary",)),
        name="moe_combine",
    )(pos.reshape(-1, 1, 2 * tmc)[:n_t], ys, route, x, mod)


def _moe(L, x, mod, g, w_router, we_gu, we_down, with_ctx):
    tmx = L.tm
    n_tok = L.N if with_ctx else L.NL
    route, counts = _router(L, x, mod, g, w_router, with_ctx)
    cnt = counts[0, :N_EXPERTS].astype(jnp.int32)
    tiles_e = (cnt + tmx - 1) // tmx
    tile_end = jnp.cumsum(tiles_e)
    start = (tile_end - tiles_e) * tmx
    n_tiles = (2 * n_tok) // tmx + N_EXPERTS
    tile_id = jnp.arange(n_tiles, dtype=jnp.int32)
    tile_expert = jnp.minimum(jnp.sum(tile_id[:, None] >= tile_end[None, :], axis=1), N_EXPERTS - 1).astype(jnp.int32)
    tile_valid = (tile_id < tile_end[-1]).astype(jnp.int32)
    rt = route[:n_tok]
    sel = jnp.stack([rt[:, R_I1], rt[:, R_I2]], axis=1).astype(jnp.int32)
    rank = jnp.stack([rt[:, R_R1], rt[:, R_R2]], axis=1).astype(jnp.int32)
    pos = (jnp.sum(jnp.where(sel[..., None] == jnp.arange(N_EXPERTS), start, 0), axis=-1) + rank).reshape(-1)
    xs = _dispatch(L, x, mod, g, pos, n_tiles * tmx, with_ctx)
    ys = _experts(L, xs, tile_expert, tile_valid, we_gu, we_down, tmx)
    return _combine(L, ys, pos, route, x, mod, with_ctx)


def _rope_tables(L, lane_dim, lane_first, lane_freq, lane_is_col, half):
    t = jnp.arange(L.T, dtype=jnp.int32)
    rows = (t // GRID_W).astype(F32)
    cols = (t % GRID_W).astype(F32)
    inv_freq = jnp.exp(-math.log(ROPE_THETA) * jnp.arange(half, dtype=F32) / half)
    pos = jnp.where(lane_is_col[None, :], cols[:, None], rows[:, None])
    ang = pos * inv_freq[lane_freq][None, :]
    cos = jnp.where(lane_dim[None, :], jnp.cos(ang), 1.0)
    sin = jnp.where(lane_dim[None, :], jnp.sin(ang), 0.0)
    sin = jnp.where(lane_first[None, :], -sin, sin)
    ident = jnp.ones((L.tm, LANES), F32)
    return jnp.concatenate([cos, ident], axis=0), jnp.concatenate([sin, 0.0 * ident], axis=0)


def _gqa_rope_tables(L):
    lane = jnp.arange(LANES)
    d = lane % HEAD_DIM
    dd = d % (HEAD_DIM // 2)
    quarter = HEAD_DIM // 4
    return _rope_tables(L, lane >= 0, dd < quarter, dd % quarter, d >= HEAD_DIM // 2, quarter)


def _mla_rope_tables(L):
    lane = jnp.arange(LANES)
    d = lane - NOPE_DIM
    in_rope = (d >= 0) & (d < ROPE_DIM)
    dd = d % (ROPE_DIM // 2)
    quarter = ROPE_DIM // 4
    return _rope_tables(L, in_rope, dd < quarter, dd % quarter, d >= ROPE_DIM // 2, quarter)


def _segment_matrix(bounds):
    lane = jnp.arange(LANES)
    m = jnp.zeros((LANES, LANES), F32)
    for lo, hi in bounds:
        inside = (lane >= lo) & (lane < hi)
        m = m + jnp.where(inside[:, None] & inside[None, :], 1.0 / (hi - lo), 0.0)
    return m.astype(BF16)


def _natten_bias(rel_bias):
    n_h = rel_bias.shape[0]
    col = jnp.arange(GRID_W, dtype=jnp.int32)
    c0 = jnp.clip(col - WIN_C // 2, 0, GRID_W - WIN_C)
    col_ok = (col[None, :] >= c0[:, None]) & (col[None, :] < c0[:, None] + WIN_C)
    col_idx = jnp.clip(col[None, :] - col[:, None] + WIN_C - 1, 0, 2 * WIN_C - 2)
    full = jnp.where(col_ok[None, None], rel_bias[:, :, col_idx], NEG_INF) * LOG2E
    masked = jnp.full((n_h, GRID_W, GRID_W), NEG_INF * LOG2E, F32)
    half = WIN_R // 2
    kinds = []
    for kind in range(3):
        q_rows = []
        for j in range(NAT_ROWS):
            ru, wu = ((j, 0), (half + j, j), (WIN_R + j, half))[kind]
            blocks = [full[:, u - ru + WIN_R - 1] if wu <= u < wu + WIN_R else masked for u in range(NAT_SPAN)]
            q_rows.append(jnp.concatenate(blocks, axis=-1))
        kinds.append(jnp.concatenate(q_rows, axis=1))
    return jnp.stack(kinds, axis=1).astype(F32)


def _dup_heads(w, n_heads):
    k = w.shape[0]
    w = w.reshape(k, n_heads, HEAD_DIM)
    return jnp.concatenate([w, w], axis=-1).reshape(k, n_heads * LANES)


def _half_pad_heads(w, n_heads, second):
    k = w.shape[0]
    w = w.reshape(k, n_heads, HEAD_DIM)
    z = jnp.zeros_like(w)
    return jnp.concatenate([z, w] if second else [w, z], axis=-1).reshape(k, n_heads * LANES)


def _ones_row(n_heads, second):
    lane = jnp.arange(LANES)
    return jnp.tile((lane == ONES_LANE[1 if second else 0]).astype(F32), n_heads)


def _alternate_heads(first, second, n_heads):
    lead = first.shape[:-1]
    f = first.reshape(lead + (n_heads, LANES))
    g = second.reshape(lead + (n_heads, LANES))
    odd = (jnp.arange(n_heads) % 2 == 1)[:, None]
    return jnp.where(odd, g, f).reshape(lead + (n_heads * LANES,))


def _logit_bound(q_sq, k_sq, scale):
    return (jnp.sqrt(q_sq * k_sq) * (scale * LOG2E * BOUND_MARGIN)).reshape(1, 1).astype(F32)


def _gain_sq(gain):
    return gain.shape[0] * jnp.max(jnp.abs(gain)) ** 2


def _gqa_mixer(L, x, mod, g_mix, p, need_ctx, tables):
    w = p["w_qkv"]
    nq = A_HEADS * HEAD_DIM
    nkv = A_KV_HEADS * HEAD_DIM
    w_v = w[:, nq + nkv:]
    w_p = jnp.concatenate([w[:, :nq], _dup_heads(w[:, nq:nq + nkv], A_KV_HEADS),
                           _half_pad_heads(w_v, A_KV_HEADS, False), _half_pad_heads(w_v, A_KV_HEADS, True)], axis=1)
    n_blk = A_KV_HEADS * LANES
    scale = HEAD_DIM ** -0.5
    gain = jnp.concatenate([jnp.tile(p["q_gain"], A_HEADS) * (scale * LOG2E), jnp.tile(p["k_gain"], 2 * A_KV_HEADS),
                            jnp.ones((2 * n_blk,), F32)]).reshape(1, -1)
    colbias = jnp.concatenate([jnp.zeros((nq + n_blk,), F32), _ones_row(A_KV_HEADS, False),
                               _ones_row(A_KV_HEADS, True)]).reshape(1, -1)
    seg = _segment_matrix([(0, HEAD_DIM), (HEAD_DIM, LANES)])
    tn = 512
    qkv = _proj(L, "gqa_qkv", x, w_p, tn=tn, mod=mod, g=g_mix, seg=seg, gain=gain, colbias=colbias,
                n_norm=(nq + n_blk) // tn, rope=(tables["gqa_cos"], tables["gqa_sin"], HEAD_DIM // 4))
    q_blocks = nq // LANES
    k_blocks = n_blk // LANES
    rep_pairs = N_PAIRS // A_KV_HEADS
    bound = _logit_bound(_gain_sq(p["q_gain"]), _gain_sq(p["k_gain"]), scale)
    return _attention(L, "gqa_attn", bound, qkv, [lambda p_: p_], qkv, [lambda p_: q_blocks + p_ // rep_pairs],
                      qkv, [lambda p_: q_blocks + k_blocks + p_ // rep_pairs,
                            lambda p_: q_blocks + 2 * k_blocks + p_ // rep_pairs],
                      pair=True, mode="all" if need_ctx else "lat")


def _natten_mixer(L, x, mod, g_mix, p, need_ctx):
    n = B_HEADS * HEAD_DIM
    w = p["w_qkv"]
    w_v = w[:, 2 * n:]
    w_p = jnp.concatenate([w[:, :2 * n], _alternate_heads(_half_pad_heads(w_v, B_HEADS, False),
                                                          _half_pad_heads(w_v, B_HEADS, True), B_HEADS)], axis=1)
    scale = HEAD_DIM ** -0.5
    n_v = B_HEADS * LANES
    gain = jnp.concatenate([jnp.tile(p["q_gain"], B_HEADS) * (scale * LOG2E), jnp.tile(p["k_gain"], B_HEADS),
                            jnp.ones((n_v,), F32)]).reshape(1, -1)
    colbias = jnp.concatenate([jnp.zeros((2 * n,), F32),
                               _alternate_heads(_ones_row(B_HEADS, False), _ones_row(B_HEADS, True), B_HEADS)]
                              ).reshape(1, -1)
    seg = _segment_matrix([(0, HEAD_DIM), (HEAD_DIM, LANES)])
    tn = 512
    qkv = _proj(L, "nat_qkv", x, w_p, tn=tn, mod=mod, g=g_mix, seg=seg, gain=gain, colbias=colbias,
                n_norm=2 * n // tn)
    qk_bound = _logit_bound(_gain_sq(p["q_gain"]), _gain_sq(p["k_gain"]), scale)
    bound = qk_bound + jnp.maximum(jnp.max(p["rel_bias"]), 0.0) * LOG2E
    o = _natten(L, bound, qkv, _natten_bias(p["rel_bias"]))
    if need_ctx:
        o_ctx = _attention(L, "nat_ctx_attn", qk_bound, qkv, [lambda p_: p_], qkv, [lambda p_: N_PAIRS + p_],
                           qkv, [lambda p_: 2 * N_PAIRS + 2 * p_, lambda p_: 2 * N_PAIRS + 2 * p_ + 1],
                           pair=True, mode="ctx")
        o = jnp.concatenate([o, o_ctx], axis=0)
    return o


def _mla_mixer(L, x, mod, g_mix, p, need_ctx, tables):
    d = L.D
    w_in = p["w_in"]
    n_c = Q_LORA + KV_LORA
    w_in_p = jnp.concatenate([w_in[:, :n_c], jnp.zeros((d, NOPE_DIM), F32), w_in[:, n_c:],
                              jnp.zeros((d, LANES - NOPE_DIM - ROPE_DIM), F32)], axis=1)
    dq = NOPE_DIM + ROPE_DIM
    pad_q = jnp.zeros((LANES - dq,), F32)
    seg_rope = _segment_matrix([(NOPE_DIM, dq)])
    gain_kr = jnp.concatenate([jnp.zeros((NOPE_DIM,), F32), p["k_gain"][NOPE_DIM:], pad_q]).reshape(1, LANES)
    cos, sin = tables["mla_cos"], tables["mla_sin"]
    cq, ckv, kr = _mla_in(L, x, mod, g_mix, w_in_p, p["g_dq"], p["g_dkv"], seg_rope, gain_kr, cos, sin)

    scale = dq ** -0.5
    w_uq = jnp.pad(p["w_uq"].reshape(Q_LORA, C_HEADS, dq), ((0, 0), (0, 0), (0, LANES - dq)))
    gain_q = jnp.tile(jnp.concatenate([p["q_gain"], pad_q]) * (scale * LOG2E), C_HEADS).reshape(1, -1)
    seg_q = _segment_matrix([(0, NOPE_DIM), (NOPE_DIM, dq)])
    n_hl = C_HEADS * LANES
    tn = 512
    q = _proj(L, "mla_q", cq, w_uq.reshape(Q_LORA, n_hl), tn=tn, seg=seg_q, gain=gain_q, n_norm=n_hl // tn,
              rope=(cos, sin, ROPE_DIM // 4))

    w_ukv = p["w_ukv"].reshape(KV_LORA, C_HEADS, NOPE_DIM + V_DIM)
    w_uk = jnp.pad(w_ukv[:, :, :NOPE_DIM], ((0, 0), (0, 0), (0, LANES - NOPE_DIM))).reshape(KV_LORA, n_hl)
    w_v = w_ukv[:, :, NOPE_DIM:].reshape(KV_LORA, C_HEADS * V_DIM)
    w_uv = _alternate_heads(_half_pad_heads(w_v, C_HEADS, False), _half_pad_heads(w_v, C_HEADS, True), C_HEADS)
    gain_k = jnp.tile(jnp.concatenate([p["k_gain"][:NOPE_DIM], jnp.zeros((LANES - NOPE_DIM,), F32)]),
                      C_HEADS).reshape(1, -1)
    seg_k = _segment_matrix([(0, NOPE_DIM)])
    k = _proj(L, "mla_k", ckv, w_uk, tn=tn, seg=seg_k, gain=gain_k, n_norm=n_hl // tn, add=kr)
    ones = _alternate_heads(_ones_row(C_HEADS, False), _ones_row(C_HEADS, True), C_HEADS).reshape(1, -1)
    v = _proj(L, "mla_v", ckv, w_uv, tn=tn, colbias=ones)
    q_sq = _gain_sq(p["q_gain"][:NOPE_DIM]) + _gain_sq(p["q_gain"][NOPE_DIM:])
    k_sq = _gain_sq(p["k_gain"][:NOPE_DIM]) + _gain_sq(p["k_gain"][NOPE_DIM:])
    bound = _logit_bound(q_sq, k_sq, scale)
    heads = [lambda p_: 2 * p_, lambda p_: 2 * p_ + 1]
    return _attention(L, "mla_attn", bound, q, heads, k, heads, v, heads, pair=False,
                      mode="all" if need_ctx else "lat")


def kernel(x, c, ctx, c_ctx, l0_w_mod, l0_b_mod, l0_g_mix, l0_g_ffn, l0_w_qkv, l0_q_gain, l0_k_gain, l0_w_o, l0_w_gu, l0_w_down, l1_w_mod, l1_b_mod, l1_g_mix, l1_g_ffn, l1_w_qkv, l1_q_gain, l1_k_gain, l1_rel_bias, l1_w_o, l1_w_router, l1_we_gu, l1_we_down, l2_w_mod, l2_b_mod, l2_g_mix, l2_g_ffn, l2_w_in, l2_g_dq, l2_g_dkv, l2_w_uq, l2_w_ukv, l2_q_gain, l2_k_gain, l2_w_o, l2_w_gu, l2_w_down, l3_w_mod, l3_b_mod, l3_g_mix, l3_g_ffn, l3_w_qkv, l3_q_gain, l3_k_gain, l3_w_o, l3_w_router, l3_we_gu, l3_we_down):
    layers = [
        dict(w_mod=l0_w_mod, b_mod=l0_b_mod, g_mix=l0_g_mix, g_ffn=l0_g_ffn, kind="gqa",
             mix=dict(w_qkv=l0_w_qkv, q_gain=l0_q_gain, k_gain=l0_k_gain), w_o=l0_w_o,
             ffn=dict(w_gu=l0_w_gu, w_down=l0_w_down)),
        dict(w_mod=l1_w_mod, b_mod=l1_b_mod, g_mix=l1_g_mix, g_ffn=l1_g_ffn, kind="natten",
             mix=dict(w_qkv=l1_w_qkv, q_gain=l1_q_gain, k_gain=l1_k_gain, rel_bias=l1_rel_bias), w_o=l1_w_o,
             moe=dict(w_router=l1_w_router, we_gu=l1_we_gu, we_down=l1_we_down)),
        dict(w_mod=l2_w_mod, b_mod=l2_b_mod, g_mix=l2_g_mix, g_ffn=l2_g_ffn, kind="mla",
             mix=dict(w_in=l2_w_in, g_dq=l2_g_dq, g_dkv=l2_g_dkv, w_uq=l2_w_uq, w_ukv=l2_w_ukv,
                      q_gain=l2_q_gain, k_gain=l2_k_gain), w_o=l2_w_o,
             ffn=dict(w_gu=l2_w_gu, w_down=l2_w_down)),
        dict(w_mod=l3_w_mod, b_mod=l3_b_mod, g_mix=l3_g_mix, g_ffn=l3_g_ffn, kind="gqa",
             mix=dict(w_qkv=l3_w_qkv, q_gain=l3_q_gain, k_gain=l3_k_gain), w_o=l3_w_o,
             moe=dict(w_router=l3_w_router, we_gu=l3_we_gu, we_down=l3_we_down)),
    ]
    batch, seq, d_model = x.shape
    L = _Layout(batch, seq, ctx.shape[1], d_model)
    gqa_cos, gqa_sin = _gqa_rope_tables(L)
    mla_cos, mla_sin = _mla_rope_tables(L)
    tables = dict(gqa_cos=gqa_cos, gqa_sin=gqa_sin, mla_cos=mla_cos, mla_sin=mla_sin)
    cvec = jnp.concatenate([c, c_ctx[None, :], jnp.zeros((SUBLANES - batch - 1, d_model), F32)], axis=0)
    xs = jnp.concatenate([x.reshape(L.NL, d_model), ctx.reshape(L.NC, d_model)], axis=0)
    for li, p in enumerate(layers):
        need_ctx = li < len(layers) - 1
        mod = _adaln(cvec, p["w_mod"], p["b_mod"], batch + 1)
        if p["kind"] == "gqa":
            o = _gqa_mixer(L, xs, mod, p["g_mix"], p["mix"], need_ctx, tables)
        elif p["kind"] == "natten":
            o = _natten_mixer(L, xs, mod, p["g_mix"], p["mix"], need_ctx)
        else:
            o = _mla_mixer(L, xs, mod, p["g_mix"], p["mix"], need_ctx, tables)
        xs = _oproj(L, o, p["w_o"], xs, mod, need_ctx)
        if "ffn" in p:
            xs = _ffn(L, xs, mod, p["g_ffn"], p["ffn"]["w_gu"], p["ffn"]["w_down"], need_ctx)
        else:
            xs = _moe(L, xs, mod, p["g_ffn"], p["moe"]["w_router"], p["moe"]["we_gu"], p["moe"]["we_down"], need_ctx)
    return xs[:L.NL].reshape(batch, seq, d_model)
```

```python
import functools
import math

import jax
import jax.numpy as jnp
from jax import lax
from jax.experimental import pallas as pl
from jax.experimental.pallas import tpu as pltpu

F32 = jnp.float32
BF16 = jnp.bfloat16

LANES = 128
SUBLANES = 8
VMEM_LIMIT_BYTES = 56 * 1024 * 1024

NORM_EPS = 1e-6
ROPE_THETA = 10000.0
NEG_INF = -1e30
GRID_W = 64
HEAD_DIM = 64
WIN_R = 8
WIN_C = 16
A_HEADS = 16
A_KV_HEADS = 4
B_HEADS = 16
C_HEADS = 16
Q_LORA = 384
KV_LORA = 256
NOPE_DIM = 64
ROPE_DIM = 32
V_DIM = 64
N_EXPERTS = 8
N_PAIRS = 8
LOG2E = math.log2(math.e)
FAST_BOUND = 40.0
BOUND_MARGIN = 1.02
ONES_LANE = (HEAD_DIM, 0)
NAT_ROWS = 4
NAT_SPAN = NAT_ROWS + WIN_R

SH1, SC1, G1, SH2, SC2, G2 = range(6)


def _params(sem):
    return pltpu.CompilerParams(dimension_semantics=sem, vmem_limit_bytes=VMEM_LIMIT_BYTES)


def _dot(a, b):
    return jnp.dot(a, b, preferred_element_type=F32)


def _dot_nt(a, b):
    return lax.dot_general(a, b, (((1,), (1,)), ((), ())), preferred_element_type=F32)


def _silu(x):
    return x * (1.0 / (1.0 + jnp.exp(-x)))


def _norm_mod(x, g, shift, scale):
    ms = jnp.mean(x * x, axis=-1, keepdims=True)
    y = x * lax.rsqrt(ms + NORM_EPS) * g
    return y * (1.0 + scale) + shift


def _seg_mean_sq(y, seg):
    return _dot((y * y).astype(BF16), seg)


def _rope(y, cos, sin_signed, shift):
    lane = lax.broadcasted_iota(jnp.int32, (1, y.shape[1]), 1)
    first = (lane & (2 * shift - 1)) < shift
    partner = jnp.where(first, pltpu.roll(y, LANES - shift, 1), pltpu.roll(y, shift, 1))
    return y * cos + partner * sin_signed


class _Layout:
    def __init__(self, batch, seq, ctx_len, d_model):
        self.B, self.T, self.TC, self.D = batch, seq, ctx_len, d_model
        self.NL = batch * seq
        self.NC = batch * ctx_len
        self.N = self.NL + self.NC
        self.tm = 1024 if (self.NC % 1024 == 0 and seq % 1024 == 0) else ctx_len
        assert seq % self.tm == 0 and self.NC % self.tm == 0
        self.tpb = seq // self.tm
        self.tq = ctx_len
        assert seq % self.tq == 0 and seq % GRID_W == 0

    def mod_group(self, i):
        return jnp.minimum(i // self.tpb, self.B)

    def rope_block(self, i):
        return jnp.where(i < self.B * self.tpb, i % self.tpb, self.tpb)

    def rows(self, with_ctx):
        return self.N if with_ctx else self.NL

    def tiles(self, with_ctx):
        return self.rows(with_ctx) // self.tm


def _adaln_kernel(c_ref, w_ref, b_ref, o_ref):
    s = _silu(c_ref[...])
    o_ref[...] = jnp.dot(s, w_ref[...], preferred_element_type=F32,
                         precision=lax.Precision.HIGHEST) + b_ref[...]


def _adaln(cvec, w_mod, b_mod, n_groups):
    d, n_out = w_mod.shape
    tn = n_out // 4
    out = pl.pallas_call(
        _adaln_kernel,
        out_shape=jax.ShapeDtypeStruct((SUBLANES, n_out), F32),
        grid=(n_out // tn,),
        in_specs=[pl.BlockSpec((SUBLANES, d), lambda j: (0, 0)),
                  pl.BlockSpec((d, tn), lambda j: (0, j)),
                  pl.BlockSpec((1, tn), lambda j: (0, j))],
        out_specs=pl.BlockSpec((SUBLANES, tn), lambda j: (0, j)),
        compiler_params=_params(("arbitrary",)),
        name="adaln",
    )(cvec, w_mod, b_mod.reshape(1, n_out))
    mod = out[:n_groups].reshape(n_groups, 6, d)
    return jnp.pad(mod, ((0, 0), (0, 2), (0, 0)))


def _proj_kernel(*refs, has_mod, n_norm, n_col_tiles, rope_shift, has_add, has_colbias, groups):
    refs = list(refs)
    x_ref = refs.pop(0)
    if has_mod:
        mod_ref, g_ref = refs.pop(0), refs.pop(0)
    w_ref = refs.pop(0)
    if n_norm:
        seg_ref, gain_ref = refs.pop(0), refs.pop(0)
    if rope_shift:
        cos_ref, sin_ref = refs.pop(0), refs.pop(0)
    if has_add:
        add_ref = refs.pop(0)
    if has_colbias:
        colbias_ref = refs.pop(0)
    o_ref = refs.pop(0)
    j = pl.program_id(1)

    if has_mod:
        h_scr = refs.pop(0)

        @pl.when(j == 0)
        def _():
            h = _norm_mod(x_ref[...], g_ref[...], mod_ref[SH1:SH1 + 1, :], mod_ref[SC1:SC1 + 1, :])
            h_scr[...] = h.astype(BF16)

        lhs_ref = h_scr
    else:
        lhs_ref = x_ref

    def matmul():
        return _dot(lhs_ref[...], w_ref[...].astype(BF16))

    def normed():
        acc = matmul()
        for c in range(groups):
            sl = slice(c * LANES, (c + 1) * LANES)
            y = acc[:, sl]
            y = y * lax.rsqrt(_seg_mean_sq(y, seg_ref[...]) + NORM_EPS) * gain_ref[:, sl]
            if rope_shift:
                y = _rope(y, cos_ref[...], sin_ref[...], rope_shift)
            if has_add:
                y = y + add_ref[...]
            o_ref[:, sl] = y.astype(o_ref.dtype)

    def plain():
        acc = matmul()
        y = acc + colbias_ref[...] if has_colbias else acc
        o_ref[...] = y.astype(o_ref.dtype)

    if n_norm == 0:
        plain()
    elif n_norm == n_col_tiles:
        normed()
    else:
        pl.when(j < n_norm)(normed)
        pl.when(j >= n_norm)(plain)


def _proj(L, name, x, w, *, with_ctx=True, tn=512, mod=None, g=None, seg=None, gain=None, n_norm=0,
          rope=None, add=None, colbias=None, out_dtype=BF16):
    k, n_out = w.shape
    tm = L.tm
    n_col = n_out // tn
    in_specs = [pl.BlockSpec((tm, k), lambda i, j: (i, 0))]
    args = [x]
    scratch = []
    if mod is not None:
        in_specs += [pl.BlockSpec((None, SUBLANES, k), lambda i, j: (L.mod_group(i), 0, 0)),
                     pl.BlockSpec((1, k), lambda i, j: (0, 0))]
        args += [mod, g.reshape(1, k)]
        scratch.append(pltpu.VMEM((tm, k), BF16))
    in_specs.append(pl.BlockSpec((k, tn), lambda i, j: (0, j)))
    args.append(w)
    if n_norm:
        in_specs += [pl.BlockSpec((LANES, LANES), lambda i, j: (0, 0)),
                     pl.BlockSpec((1, tn), lambda i, j: (0, j))]
        args += [seg, gain]
    rope_shift = 0
    if rope is not None:
        cos, sin, rope_shift = rope
        in_specs += [pl.BlockSpec((tm, LANES), lambda i, j: (L.rope_block(i), 0))] * 2
        args += [cos, sin]
    if add is not None:
        in_specs.append(pl.BlockSpec((tm, LANES), lambda i, j: (i, 0)))
        args.append(add)
    if colbias is not None:
        in_specs.append(pl.BlockSpec((1, tn), lambda i, j: (0, j)))
        args.append(colbias)
    kern = functools.partial(_proj_kernel, has_mod=mod is not None, n_norm=n_norm, n_col_tiles=n_col,
                             rope_shift=rope_shift, has_add=add is not None, has_colbias=colbias is not None,
                             groups=tn // LANES)
    return pl.pallas_call(
        kern,
        out_shape=jax.ShapeDtypeStruct((L.N, n_out), out_dtype),
        grid=(L.tiles(with_ctx), n_col),
        in_specs=in_specs,
        out_specs=pl.BlockSpec((tm, tn), lambda i, j: (i, j)),
        scratch_shapes=scratch,
        compiler_params=_params(("parallel", "arbitrary")),
        name=name,
    )(*args)


def _mla_in_kernel(x_ref, mod_ref, g_ref, w_ref, gdq_ref, gdkv_ref, seg_ref, gain_ref, cos_ref, sin_ref,
                   cq_ref, ckv_ref, kr_ref):
    h = _norm_mod(x_ref[...], g_ref[...], mod_ref[SH1:SH1 + 1, :], mod_ref[SC1:SC1 + 1, :])
    acc = _dot(h.astype(BF16), w_ref[...].astype(BF16))
    cq = acc[:, :Q_LORA]
    cq = cq * lax.rsqrt(jnp.mean(cq * cq, axis=-1, keepdims=True) + NORM_EPS) * gdq_ref[...]
    cq_ref[...] = cq.astype(BF16)
    ckv = acc[:, Q_LORA:Q_LORA + KV_LORA]
    ckv = ckv * lax.rsqrt(jnp.mean(ckv * ckv, axis=-1, keepdims=True) + NORM_EPS) * gdkv_ref[...]
    ckv_ref[...] = ckv.astype(BF16)
    kr = acc[:, Q_LORA + KV_LORA:]
    kr = kr * lax.rsqrt(_seg_mean_sq(kr, seg_ref[...]) + NORM_EPS) * gain_ref[...]
    kr_ref[...] = _rope(kr, cos_ref[...], sin_ref[...], ROPE_DIM // 4)


def _mla_in(L, x, mod, g, w_in_p, g_dq, g_dkv, seg_kr, gain_kr, cos, sin):
    tm, d = L.tm, L.D
    n_out = w_in_p.shape[1]
    full = lambda shape: pl.BlockSpec(shape, lambda i: (0,) * len(shape))
    return pl.pallas_call(
        _mla_in_kernel,
        out_shape=(jax.ShapeDtypeStruct((L.N, Q_LORA), BF16),
                   jax.ShapeDtypeStruct((L.N, KV_LORA), BF16),
                   jax.ShapeDtypeStruct((L.N, LANES), F32)),
        grid=(L.tiles(True),),
        in_specs=[pl.BlockSpec((tm, d), lambda i: (i, 0)),
                  pl.BlockSpec((None, SUBLANES, d), lambda i: (L.mod_group(i), 0, 0)),
                  full((1, d)), full((d, n_out)), full((1, Q_LORA)), full((1, KV_LORA)),
                  full((LANES, LANES)), full((1, LANES)),
                  pl.BlockSpec((tm, LANES), lambda i: (L.rope_block(i), 0)),
                  pl.BlockSpec((tm, LANES), lambda i: (L.rope_block(i), 0))],
        out_specs=(pl.BlockSpec((tm, Q_LORA), lambda i: (i, 0)),
                   pl.BlockSpec((tm, KV_LORA), lambda i: (i, 0)),
                   pl.BlockSpec((tm, LANES), lambda i: (i, 0))),
        compiler_params=_params(("parallel",)),
        name="mla_in",
    )(x, mod, g.reshape(1, d), w_in_p, g_dq.reshape(1, -1), g_dkv.reshape(1, -1), seg_kr, gain_kr, cos, sin)


def _softmax_pv(s_list, v_list, bound, ones_lane, fast):
    if fast:
        o = None
        for s, v in zip(s_list, v_list):
            part = _dot(jnp.exp2(s - bound).astype(BF16), v)
            o = part if o is None else o + part
        return o / o[:, ones_lane:ones_lane + 1]
    m = None
    for s in s_list:
        ms = jnp.max(s, axis=-1, keepdims=True)
        m = ms if m is None else jnp.maximum(m, ms)
    o, den = None, None
    for s, v in zip(s_list, v_list):
        p = jnp.exp2(s - m)
        ds = jnp.sum(p, axis=-1, keepdims=True)
        part = _dot(p.astype(BF16), v)
        o = part if o is None else o + part
        den = ds if den is None else den + ds
    return o / den


def _attn_kernel(*refs, nq, nk, pair, lat_keys, nq_lat, ctx_q):
    refs = list(refs)
    bound_ref = refs.pop(0)
    q_refs = [refs.pop(0) for _ in range(nq)]
    kl_refs = [refs.pop(0) for _ in range(nk)] if lat_keys else []
    kc_refs = [refs.pop(0) for _ in range(nk)]
    vl_refs = [refs.pop(0) for _ in range(2)] if lat_keys else []
    vc_refs = [refs.pop(0) for _ in range(2)]
    o_ref = refs.pop(0)
    lo = lax.broadcasted_iota(jnp.int32, (1, LANES), 1) < HEAD_DIM
    bound = bound_ref[0, 0]

    def run(use_lat, fast):
        outs = []
        for hh in range(2):
            q = q_refs[hh % nq][...]
            if pair:
                q = jnp.where(lo if hh == 0 else jnp.logical_not(lo), q, jnp.zeros_like(q))
            s_list = [_dot_nt(q, kc_refs[hh % nk][...])]
            v_list = [vc_refs[hh][...]]
            if use_lat:
                s_list.append(_dot_nt(q, kl_refs[hh % nk][...]))
                v_list.append(vl_refs[hh][...])
            outs.append(_softmax_pv(s_list, v_list, bound, ONES_LANE[hh], fast))
        o_ref[...] = jnp.where(lo, outs[0], outs[1]).astype(o_ref.dtype)

    is_fast = bound <= FAST_BOUND
    for fast in (True, False):
        pred = is_fast if fast else jnp.logical_not(is_fast)
        if lat_keys and ctx_q:
            i = pl.program_id(2)
            pl.when(jnp.logical_and(pred, i < nq_lat))(functools.partial(run, True, fast))
            pl.when(jnp.logical_and(pred, i == nq_lat))(functools.partial(run, False, fast))
        else:
            pl.when(pred)(functools.partial(run, lat_keys, fast))


def _attention(L, name, bound, q, q_cols, k, k_cols, v, v_cols, *, pair, mode):
    B, T, TC, tq = L.B, L.T, L.TC, L.tq
    nq_lat = T // tq
    ctx_row = L.NL // TC
    lat_keys = mode != "ctx"
    ctx_q = mode != "lat"
    n_i = {"all": nq_lat + 1, "lat": nq_lat, "ctx": 1}[mode]

    def q_row(b, i):
        if mode == "ctx":
            return ctx_row + b
        if mode == "lat":
            return b * nq_lat + i
        return jnp.where(i < nq_lat, b * nq_lat + i, ctx_row + b)

    in_specs = [pl.BlockSpec(memory_space=pltpu.SMEM)]
    args = [bound]
    for f in q_cols:
        in_specs.append(pl.BlockSpec((tq, LANES), lambda b, p, i, f=f: (q_row(b, i), f(p))))
        args.append(q)
    if lat_keys:
        for f in k_cols:
            in_specs.append(pl.BlockSpec((T, LANES), lambda b, p, i, f=f: (b, f(p))))
            args.append(k)
    for f in k_cols:
        in_specs.append(pl.BlockSpec((TC, LANES), lambda b, p, i, f=f: (ctx_row + b, f(p))))
        args.append(k)
    if lat_keys:
        for f in v_cols:
            in_specs.append(pl.BlockSpec((T, LANES), lambda b, p, i, f=f: (b, f(p))))
            args.append(v)
    for f in v_cols:
        in_specs.append(pl.BlockSpec((TC, LANES), lambda b, p, i, f=f: (ctx_row + b, f(p))))
        args.append(v)
    out_rows = {"all": L.N, "lat": L.NL, "ctx": L.NC}[mode]
    out_row = (lambda b, i: b) if mode == "ctx" else q_row
    kern = functools.partial(_attn_kernel, nq=len(q_cols), nk=len(k_cols), pair=pair, lat_keys=lat_keys,
                             nq_lat=nq_lat, ctx_q=ctx_q)
    return pl.pallas_call(
        kern,
        out_shape=jax.ShapeDtypeStruct((out_rows, N_PAIRS * LANES), BF16),
        grid=(B, N_PAIRS, n_i),
        in_specs=in_specs,
        out_specs=pl.BlockSpec((tq, LANES), lambda b, p, i: (out_row(b, i), p)),
        compiler_params=_params(("parallel", "parallel", "arbitrary")),
        name=name,
    )(*args)


def _natten_kernel(bound_ref, q_ref, k_ref, va_ref, vb_ref, kc_ref, vca_ref, vcb_ref, bias_ref, o_ref, oc_scr, *,
                   rows_n):
    lo = lax.broadcasted_iota(jnp.int32, (1, LANES), 1) < HEAD_DIM
    halves = (lo, jnp.logical_not(lo))
    v_refs = (va_ref, vb_ref)
    vc_refs = (vca_ref, vcb_ref)
    n_groups = rows_n // NAT_ROWS
    bound = bound_ref[0, 0]

    def group(g):
        u0 = jnp.clip(g * NAT_ROWS - WIN_R // 2, 0, rows_n - NAT_SPAN)
        kind = jnp.where(g == 0, 0, jnp.where(g == n_groups - 1, 2, 1))
        rows = pl.ds(pl.multiple_of(g * (NAT_ROWS * GRID_W), NAT_ROWS * GRID_W), NAT_ROWS * GRID_W)
        keys = pl.ds(pl.multiple_of(u0 * GRID_W, GRID_W), NAT_SPAN * GRID_W)
        return kind, rows, keys

    def masked(q, hh):
        return jnp.where(halves[hh], q, jnp.zeros_like(q))

    def fast_body(g, carry):
        kind, rows, keys = group(g)
        q = q_ref[rows, :]
        kw = k_ref[keys, :]
        outs = []
        for hh in range(2):
            s_l = _dot_nt(masked(q, hh), kw) + bias_ref[hh, kind]
            o = oc_scr[hh, rows, :] + _dot(jnp.exp2(s_l - bound).astype(BF16), v_refs[hh][keys, :])
            outs.append(o / o[:, ONES_LANE[hh]:ONES_LANE[hh] + 1])
        o_ref[rows, :] = jnp.where(lo, outs[0], outs[1]).astype(o_ref.dtype)
        return carry

    def exact_body(g, carry):
        kind, rows, keys = group(g)
        q = q_ref[rows, :]
        kw = k_ref[keys, :]
        outs = []
        for hh in range(2):
            qh = masked(q, hh)
            s_l = _dot_nt(qh, kw) + bias_ref[hh, kind]
            s_c = _dot_nt(qh, kc_ref[...])
            outs.append(_softmax_pv([s_l, s_c], [v_refs[hh][keys, :], vc_refs[hh][...]], bound, ONES_LANE[hh], False))
        o_ref[rows, :] = jnp.where(lo, outs[0], outs[1]).astype(o_ref.dtype)
        return carry

    is_fast = bound <= FAST_BOUND

    @pl.when(is_fast)
    def _():
        q_all = q_ref[...]
        for hh in range(2):
            p_c = jnp.exp2(_dot_nt(masked(q_all, hh), kc_ref[...]) - bound).astype(BF16)
            oc_scr[hh] = _dot(p_c, vc_refs[hh][...])
        lax.fori_loop(0, n_groups, fast_body, 0, unroll=2)

    @pl.when(jnp.logical_not(is_fast))
    def _():
        lax.fori_loop(0, n_groups, exact_body, 0)


def _natten(L, bound, qkv, bias):
    B, T, TC = L.B, L.T, L.TC
    ctx_row = L.NL // TC
    rows_n = T // GRID_W
    assert rows_n % NAT_ROWS == 0 and rows_n >= NAT_SPAN and NAT_ROWS == WIN_R // 2
    k0, v0 = N_PAIRS, 2 * N_PAIRS
    return pl.pallas_call(
        functools.partial(_natten_kernel, rows_n=T // GRID_W),
        out_shape=jax.ShapeDtypeStruct((L.NL, N_PAIRS * LANES), BF16),
        grid=(B, N_PAIRS),
        in_specs=[pl.BlockSpec(memory_space=pltpu.SMEM),
                  pl.BlockSpec((T, LANES), lambda b, p: (b, p)),
                  pl.BlockSpec((T, LANES), lambda b, p: (b, k0 + p)),
                  pl.BlockSpec((T, LANES), lambda b, p: (b, v0 + 2 * p)),
                  pl.BlockSpec((T, LANES), lambda b, p: (b, v0 + 2 * p + 1)),
                  pl.BlockSpec((TC, LANES), lambda b, p: (ctx_row + b, k0 + p)),
                  pl.BlockSpec((TC, LANES), lambda b, p: (ctx_row + b, v0 + 2 * p)),
                  pl.BlockSpec((TC, LANES), lambda b, p: (ctx_row + b, v0 + 2 * p + 1)),
                  pl.BlockSpec((2, 3, NAT_ROWS * GRID_W, NAT_SPAN * GRID_W), lambda b, p: (p, 0, 0, 0))],
        out_specs=pl.BlockSpec((T, LANES), lambda b, p: (b, p)),
        scratch_shapes=[pltpu.VMEM((2, T, LANES), F32)],
        compiler_params=_params(("parallel", "parallel")),
        name="natten",
    )(bound, qkv, qkv, qkv, qkv, qkv, qkv, qkv, bias)


def _oproj_kernel(o_ref, w_ref, x_ref, mod_ref, out_ref):
    acc = _dot(o_ref[...], w_ref[...].astype(BF16))
    out_ref[...] = x_ref[...] + mod_ref[G1:G1 + 1, :] * acc


def _oproj(L, o, w_o, x, mod, with_ctx):
    tm, d = L.tm // 2, L.D
    k = w_o.shape[0]
    ratio = L.tm // tm
    return pl.pallas_call(
        _oproj_kernel,
        out_shape=jax.ShapeDtypeStruct((L.rows(with_ctx), d), F32),
        grid=(L.tiles(with_ctx) * ratio,),
        in_specs=[pl.BlockSpec((tm, k), lambda i: (i, 0)),
                  pl.BlockSpec((k, d), lambda i: (0, 0)),
                  pl.BlockSpec((tm, d), lambda i: (i, 0)),
                  pl.BlockSpec((None, SUBLANES, d), lambda i: (L.mod_group(i // ratio), 0, 0))],
        out_specs=pl.BlockSpec((tm, d), lambda i: (i, 0)),
        compiler_params=_params(("parallel",)),
        name="oproj",
    )(o, w_o, x, mod)


def _ffn_kernel(x_ref, mod_ref, g_ref, wgu_ref, wd_ref, out_ref, h_scr, acc_scr, *, d_ff, tf):
    h = _norm_mod(x_ref[...], g_ref[...], mod_ref[SH2:SH2 + 1, :], mod_ref[SC2:SC2 + 1, :])
    h_scr[...] = h.astype(BF16)
    for c in range(d_ff // tf):
        h = h_scr[...]
        gate = _dot(h, wgu_ref[:, c * tf:(c + 1) * tf])
        up = _dot(h, wgu_ref[:, d_ff + c * tf:d_ff + (c + 1) * tf])
        act = (_silu(gate) * up).astype(BF16)
        part = _dot(act, wd_ref[c * tf:(c + 1) * tf, :])
        if c == 0:
            acc_scr[...] = part
        else:
            acc_scr[...] += part
    out_ref[...] = x_ref[...] + mod_ref[G2:G2 + 1, :] * acc_scr[...]


def _ffn(L, x, mod, g, w_gu, w_down, with_ctx):
    tm, d = L.tm, L.D
    d_ff = w_down.shape[0]
    resident = lambda shape: pl.BlockSpec(shape, lambda i: (0, 0), pipeline_mode=pl.Buffered(1))
    return pl.pallas_call(
        functools.partial(_ffn_kernel, d_ff=d_ff, tf=256),
        out_shape=jax.ShapeDtypeStruct((L.rows(with_ctx), d), F32),
        grid=(L.tiles(with_ctx),),
        in_specs=[pl.BlockSpec((tm, d), lambda i: (i, 0)),
                  pl.BlockSpec((None, SUBLANES, d), lambda i: (L.mod_group(i), 0, 0)),
                  pl.BlockSpec((1, d), lambda i: (0, 0)),
                  resident((d, 2 * d_ff)),
                  resident((d_ff, d))],
        out_specs=pl.BlockSpec((tm, d), lambda i: (i, 0)),
        scratch_shapes=[pltpu.VMEM((tm, d), BF16), pltpu.VMEM((tm, d), F32)],
        compiler_params=_params(("parallel",)),
        name="ffn",
    )(x, mod, g.reshape(1, d), w_gu.astype(BF16), w_down.astype(BF16))


R_I1, R_I2, R_W1, R_W2, R_R1, R_R2 = range(6)


def _router_kernel(x_ref, mod_ref, g_ref, wr_ref, route_ref, cnt_ref, carry_scr, *, tm):
    i = pl.program_id(0)

    @pl.when(i == 0)
    def _():
        carry_scr[...] = jnp.zeros_like(carry_scr)

    h = _norm_mod(x_ref[...], g_ref[...], mod_ref[SH2:SH2 + 1, :], mod_ref[SC2:SC2 + 1, :])
    w = wr_ref[...]
    h_hi, w_hi = h.astype(BF16), w.astype(BF16)
    h_lo, w_lo = (h - h_hi.astype(F32)).astype(BF16), (w - w_hi.astype(F32)).astype(BF16)
    logits = _dot(h_hi, w_hi) + (_dot(h_lo, w_hi) + _dot(h_hi, w_lo))
    lane = lax.broadcasted_iota(jnp.int32, logits.shape, 1).astype(F32)
    lg = jnp.where(lane < N_EXPERTS, logits, -jnp.inf)
    m1 = jnp.max(lg, axis=-1, keepdims=True)
    i1 = jnp.min(jnp.where(lg == m1, lane, float(LANES)), axis=-1, keepdims=True)
    lg2 = jnp.where(lane == i1, -jnp.inf, lg)
    m2 = jnp.max(lg2, axis=-1, keepdims=True)
    i2 = jnp.min(jnp.where(lg2 == m2, lane, float(LANES)), axis=-1, keepdims=True)
    e = jnp.exp(m2 - m1)
    w1 = 1.0 / (1.0 + e)
    w2 = e / (1.0 + e)
    onehot = jnp.where((lane == i1) | (lane == i2), 1.0, 0.0)
    row = lax.broadcasted_iota(jnp.int32, (tm, tm), 0)
    col = lax.broadcasted_iota(jnp.int32, (tm, tm), 1)
    tri = jnp.where(row > col, 1.0, 0.0).astype(BF16)
    rank = _dot(tri, onehot.astype(BF16)) + carry_scr[0:1, :]
    r1 = jnp.sum(jnp.where(lane == i1, rank, 0.0), axis=-1, keepdims=True)
    r2 = jnp.sum(jnp.where(lane == i2, rank, 0.0), axis=-1, keepdims=True)
    carry_scr[...] = carry_scr[...] + jnp.sum(onehot, axis=0, keepdims=True)
    rec = jnp.zeros_like(logits)
    for pos, val in ((R_I1, i1), (R_I2, i2), (R_W1, w1), (R_W2, w2), (R_R1, r1), (R_R2, r2)):
        rec = jnp.where(lane == float(pos), val, rec)
    route_ref[...] = rec
    cnt_ref[...] = carry_scr[...]


def _router(L, x, mod, g, w_router, with_ctx):
    tm, d = L.tm, L.D
    wr = jnp.pad(w_router, ((0, 0), (0, LANES - w_router.shape[1])))
    return pl.pallas_call(
        functools.partial(_router_kernel, tm=tm),
        out_shape=(jax.ShapeDtypeStruct((L.rows(with_ctx), LANES), F32),
                   jax.ShapeDtypeStruct((SUBLANES, LANES), F32)),
        grid=(L.tiles(with_ctx),),
        in_specs=[pl.BlockSpec((tm, d), lambda i: (i, 0)),
                  pl.BlockSpec((None, SUBLANES, d), lambda i: (L.mod_group(i), 0, 0)),
                  pl.BlockSpec((1, d), lambda i: (0, 0)),
                  pl.BlockSpec((d, LANES), lambda i: (0, 0))],
        out_specs=(pl.BlockSpec((tm, LANES), lambda i: (i, 0)),
                   pl.BlockSpec((SUBLANES, LANES), lambda i: (0, 0))),
        scratch_shapes=[pltpu.VMEM((SUBLANES, LANES), F32)],
        compiler_params=_params(("arbitrary",)),
        name="router",
    )(x, mod, g.reshape(1, d), wr)


def _row_block(ref, row):
    return ref.at[pl.ds(pl.multiple_of(row * SUBLANES, SUBLANES), SUBLANES)]


def _dispatch_kernel(pos_ref, x_ref, mod_ref, g_ref, xs_in_ref, xs_ref, rows_scr, sem, *, tmd):
    del xs_in_ref
    h = _norm_mod(x_ref[...], g_ref[...], mod_ref[SH2:SH2 + 1, :], mod_ref[SC2:SC2 + 1, :])
    for c in range(h.shape[1] // LANES):
        rows_scr[pl.ds(c, tmd, stride=SUBLANES), :] = h[:, c * LANES:(c + 1) * LANES]

    def copies(t):
        src = _row_block(rows_scr, t)
        return (pltpu.make_async_copy(src, _row_block(xs_ref, pos_ref[0, 2 * t]), sem),
                pltpu.make_async_copy(src, _row_block(xs_ref, pos_ref[0, 2 * t + 1]), sem))

    def start(t, carry):
        for k, cp in enumerate(copies(t)):
            cp.start(priority=k)
        return carry

    def wait(t, carry):
        for cp in copies(t):
            cp.wait()
        return carry

    lax.fori_loop(0, tmd, start, 0)
    lax.fori_loop(0, tmd, wait, 0)


def _dispatch(L, x, mod, g, pos, n_slots, with_ctx):
    tmd, d = L.tm, L.D
    n_t = L.tiles(with_ctx)
    xs0 = jnp.zeros((n_slots * SUBLANES, LANES), F32)
    return pl.pallas_call(
        functools.partial(_dispatch_kernel, tmd=tmd),
        out_shape=jax.ShapeDtypeStruct(xs0.shape, F32),
        grid=(n_t,),
        in_specs=[pl.BlockSpec((None, 1, 2 * tmd), lambda i: (i, 0, 0), memory_space=pltpu.SMEM),
                  pl.BlockSpec((tmd, d), lambda i: (i, 0)),
                  pl.BlockSpec((None, SUBLANES, d), lambda i: (L.mod_group(i), 0, 0)),
                  pl.BlockSpec((1, d), lambda i: (0, 0)),
                  pl.BlockSpec(memory_space=pl.ANY)],
        out_specs=pl.BlockSpec(memory_space=pl.ANY),
        scratch_shapes=[pltpu.VMEM((tmd * SUBLANES, LANES), F32), pltpu.SemaphoreType.DMA(())],
        input_output_aliases={4: 0},
        compiler_params=pltpu.CompilerParams(dimension_semantics=("arbitrary",), has_side_effects=True,
                                             vmem_limit_bytes=VMEM_LIMIT_BYTES),
        name="moe_dispatch",
    )(pos.reshape(-1, 1, 2 * tmd)[:n_t], x, mod, g.reshape(1, d), xs0)


def _experts_kernel(te_ref, tv_ref, xs_ref, wg_ref, wu_ref, wd_ref, ys_ref, h_scr, acc_scr, *, tmx):
    del te_ref
    t, c = pl.program_id(0), pl.program_id(1)
    valid = tv_ref[t] > 0
    groups = h_scr.shape[1] // LANES

    @pl.when(jnp.logical_and(valid, c == 0))
    def _():
        for k in range(groups):
            h_scr[:, k * LANES:(k + 1) * LANES] = xs_ref[pl.ds(k, tmx, stride=SUBLANES), :].astype(BF16)
        acc_scr[...] = jnp.zeros_like(acc_scr)

    @pl.when(valid)
    def _():
        h = h_scr[...]
        gate = _dot(h, wg_ref[...].astype(BF16))
        up = _dot(h, wu_ref[...].astype(BF16))
        act = (_silu(gate) * up).astype(BF16)
        acc_scr[...] += _dot(act, wd_ref[...].astype(BF16))

    @pl.when(jnp.logical_and(valid, c == pl.num_programs(1) - 1))
    def _():
        for k in range(groups):
            ys_ref[pl.ds(k, tmx, stride=SUBLANES), :] = acc_scr[:, k * LANES:(k + 1) * LANES]

    @pl.when(jnp.logical_and(jnp.logical_not(valid), c == 0))
    def _():
        ys_ref[...] = jnp.zeros_like(ys_ref)


def _experts(L, xs, tile_expert, tile_valid, we_gu, we_down, tmx):
    d = L.D
    d_ff = we_down.shape[1]
    tf = 512
    n_ch = d_ff // tf
    n_tiles = xs.shape[0] // (tmx * SUBLANES)

    def chunk(c, tv, t):
        return jnp.where(tv[t] > 0, c, n_ch - 1)

    grid_spec = pltpu.PrefetchScalarGridSpec(
        num_scalar_prefetch=2,
        grid=(n_tiles, n_ch),
        in_specs=[pl.BlockSpec((tmx * SUBLANES, LANES), lambda t, c, te, tv: (t, 0)),
                  pl.BlockSpec((None, d, tf), lambda t, c, te, tv: (te[t], 0, chunk(c, tv, t))),
                  pl.BlockSpec((None, d, tf), lambda t, c, te, tv: (te[t], 0, n_ch + chunk(c, tv, t))),
                  pl.BlockSpec((None, tf, d), lambda t, c, te, tv: (te[t], chunk(c, tv, t), 0))],
        out_specs=pl.BlockSpec((tmx * SUBLANES, LANES), lambda t, c, te, tv: (t, 0)),
        scratch_shapes=[pltpu.VMEM((tmx, d), BF16), pltpu.VMEM((tmx, d), F32)],
    )
    return pl.pallas_call(
        functools.partial(_experts_kernel, tmx=tmx),
        out_shape=jax.ShapeDtypeStruct(xs.shape, F32),
        grid_spec=grid_spec,
        compiler_params=_params(("arbitrary", "arbitrary")),
        name="moe_experts",
    )(tile_expert, tile_valid, xs, we_gu, we_gu, we_down)


def _combine_kernel(pos_ref, ys_ref, route_ref, x_ref, mod_ref, out_ref, buf1, buf2, sem, *, tmc):
    def copies(t):
        dst1 = _row_block(buf1, t)
        dst2 = _row_block(buf2, t)
        return (pltpu.make_async_copy(_row_block(ys_ref, pos_ref[0, 2 * t]), dst1, sem),
                pltpu.make_async_copy(_row_block(ys_ref, pos_ref[0, 2 * t + 1]), dst2, sem))

    def start(t, carry):
        for k, cp in enumerate(copies(t)):
            cp.start(priority=k)
        return carry

    def wait(t, carry):
        for cp in copies(t):
            cp.wait()
        return carry

    lax.fori_loop(0, tmc, start, 0)
    lax.fori_loop(0, tmc, wait, 0)
    w1 = route_ref[:, R_W1:R_W1 + 1]
    w2 = route_ref[:, R_W2:R_W2 + 1]
    for k in range(out_ref.shape[1] // LANES):
        sl = slice(k * LANES, (k + 1) * LANES)
        y = w1 * buf1[pl.ds(k, tmc, stride=SUBLANES), :] + w2 * buf2[pl.ds(k, tmc, stride=SUBLANES), :]
        out_ref[:, sl] = x_ref[:, sl] + mod_ref[G2:G2 + 1, sl] * y


def _combine(L, ys, pos, route, x, mod, with_ctx):
    tmc, d = L.TC, L.D
    ratio = L.tm // tmc
    n_t = L.tiles(with_ctx) * ratio
    return pl.pallas_call(
        functools.partial(_combine_kernel, tmc=tmc),
        out_shape=jax.ShapeDtypeStruct((L.rows(with_ctx), d), F32),
        grid=(n_t,),
        in_specs=[pl.BlockSpec((None, 1, 2 * tmc), lambda i: (i, 0, 0), memory_space=pltpu.SMEM),
                  pl.BlockSpec(memory_space=pl.ANY),
                  pl.BlockSpec((tmc, LANES), lambda i: (i, 0)),
                  pl.BlockSpec((tmc, d), lambda i: (i, 0)),
                  pl.BlockSpec((None, SUBLANES, d), lambda i: (L.mod_group(i // ratio), 0, 0))],
        out_specs=pl.BlockSpec((tmc, d), lambda i: (i, 0)),
        scratch_shapes=[pltpu.VMEM((tmc * SUBLANES, LANES), F32), pltpu.VMEM((tmc * SUBLANES, LANES), F32),
                        pltpu.SemaphoreType.DMA(())],
        compiler_params=_params(("arbitrary",)),
        name="moe_combine",
    )(pos.reshape(-1, 1, 2 * tmc)[:n_t], ys, route, x, mod)


def _moe(L, x, mod, g, w_router, we_gu, we_down, with_ctx):
    tmx = L.tm
    n_tok = L.N if with_ctx else L.NL
    route, counts = _router(L, x, mod, g, w_router, with_ctx)
    cnt = counts[0, :N_EXPERTS].astype(jnp.int32)
    tiles_e = (cnt + tmx - 1) // tmx
    tile_end = jnp.cumsum(tiles_e)
    start = (tile_end - tiles_e) * tmx
    n_tiles = (2 * n_tok) // tmx + N_EXPERTS
    tile_id = jnp.arange(n_tiles, dtype=jnp.int32)
    tile_expert = jnp.minimum(jnp.sum(tile_id[:, None] >= tile_end[None, :], axis=1), N_EXPERTS - 1).astype(jnp.int32)
    tile_valid = (tile_id < tile_end[-1]).astype(jnp.int32)
    rt = route[:n_tok]
    sel = jnp.stack([rt[:, R_I1], rt[:, R_I2]], axis=1).astype(jnp.int32)
    rank = jnp.stack([rt[:, R_R1], rt[:, R_R2]], axis=1).astype(jnp.int32)
    pos = (jnp.sum(jnp.where(sel[..., None] == jnp.arange(N_EXPERTS), start, 0), axis=-1) + rank).reshape(-1)
    xs = _dispatch(L, x, mod, g, pos, n_tiles * tmx, with_ctx)
    ys = _experts(L, xs, tile_expert, tile_valid, we_gu, we_down, tmx)
    return _combine(L, ys, pos, route, x, mod, with_ctx)


def _rope_tables(L, lane_dim, lane_first, lane_freq, lane_is_col, half):
    t = jnp.arange(L.T, dtype=jnp.int32)
    rows = (t // GRID_W).astype(F32)
    cols = (t % GRID_W).astype(F32)
    inv_freq = jnp.exp(-math.log(ROPE_THETA) * jnp.arange(half, dtype=F32) / half)
    pos = jnp.where(lane_is_col[None, :], cols[:, None], rows[:, None])
    ang = pos * inv_freq[lane_freq][None, :]
    cos = jnp.where(lane_dim[None, :], jnp.cos(ang), 1.0)
    sin = jnp.where(lane_dim[None, :], jnp.sin(ang), 0.0)
    sin = jnp.where(lane_first[None, :], -sin, sin)
    ident = jnp.ones((L.tm, LANES), F32)
    return jnp.concatenate([cos, ident], axis=0), jnp.concatenate([sin, 0.0 * ident], axis=0)


def _gqa_rope_tables(L):
    lane = jnp.arange(LANES)
    d = lane % HEAD_DIM
    dd = d % (HEAD_DIM // 2)
    quarter = HEAD_DIM // 4
    return _rope_tables(L, lane >= 0, dd < quarter, dd % quarter, d >= HEAD_DIM // 2, quarter)


def _mla_rope_tables(L):
    lane = jnp.arange(LANES)
    d = lane - NOPE_DIM
    in_rope = (d >= 0) & (d < ROPE_DIM)
    dd = d % (ROPE_DIM // 2)
    quarter = ROPE_DIM // 4
    return _rope_tables(L, in_rope, dd < quarter, dd % quarter, d >= ROPE_DIM // 2, quarter)


def _segment_matrix(bounds):
    lane = jnp.arange(LANES)
    m = jnp.zeros((LANES, LANES), F32)
    for lo, hi in bounds:
        inside = (lane >= lo) & (lane < hi)
        m = m + jnp.where(inside[:, None] & inside[None, :], 1.0 / (hi - lo), 0.0)
    return m.astype(BF16)


def _natten_bias(rel_bias):
    n_h = rel_bias.shape[0]
    col = jnp.arange(GRID_W, dtype=jnp.int32)
    c0 = jnp.clip(col - WIN_C // 2, 0, GRID_W - WIN_C)
    col_ok = (col[None, :] >= c0[:, None]) & (col[None, :] < c0[:, None] + WIN_C)
    col_idx = jnp.clip(col[None, :] - col[:, None] + WIN_C - 1, 0, 2 * WIN_C - 2)
    full = jnp.where(col_ok[None, None], rel_bias[:, :, col_idx], NEG_INF) * LOG2E
    masked = jnp.full((n_h, GRID_W, GRID_W), NEG_INF * LOG2E, F32)
    half = WIN_R // 2
    kinds = []
    for kind in range(3):
        q_rows = []
        for j in range(NAT_ROWS):
            ru, wu = ((j, 0), (half + j, j), (WIN_R + j, half))[kind]
            blocks = [full[:, u - ru + WIN_R - 1] if wu <= u < wu + WIN_R else masked for u in range(NAT_SPAN)]
            q_rows.append(jnp.concatenate(blocks, axis=-1))
        kinds.append(jnp.concatenate(q_rows, axis=1))
    return jnp.stack(kinds, axis=1).astype(F32)


def _dup_heads(w, n_heads):
    k = w.shape[0]
    w = w.reshape(k, n_heads, HEAD_DIM)
    return jnp.concatenate([w, w], axis=-1).reshape(k, n_heads * LANES)


def _half_pad_heads(w, n_heads, second):
    k = w.shape[0]
    w = w.reshape(k, n_heads, HEAD_DIM)
    z = jnp.zeros_like(w)
    return jnp.concatenate([z, w] if second else [w, z], axis=-1).reshape(k, n_heads * LANES)


def _ones_row(n_heads, second):
    lane = jnp.arange(LANES)
    return jnp.tile((lane == ONES_LANE[1 if second else 0]).astype(F32), n_heads)


def _alternate_heads(first, second, n_heads):
    lead = first.shape[:-1]
    f = first.reshape(lead + (n_heads, LANES))
    g = second.reshape(lead + (n_heads, LANES))
    odd = (jnp.arange(n_heads) % 2 == 1)[:, None]
    return jnp.where(odd, g, f).reshape(lead + (n_heads * LANES,))


def _logit_bound(q_sq, k_sq, scale):
    return (jnp.sqrt(q_sq * k_sq) * (scale * LOG2E * BOUND_MARGIN)).reshape(1, 1).astype(F32)


def _gain_sq(gain):
    return gain.shape[0] * jnp.max(jnp.abs(gain)) ** 2


def _gqa_mixer(L, x, mod, g_mix, p, need_ctx, tables):
    w = p["w_qkv"]
    nq = A_HEADS * HEAD_DIM
    nkv = A_KV_HEADS * HEAD_DIM
    w_v = w[:, nq + nkv:]
    w_p = jnp.concatenate([w[:, :nq], _dup_heads(w[:, nq:nq + nkv], A_KV_HEADS),
                           _half_pad_heads(w_v, A_KV_HEADS, False), _half_pad_heads(w_v, A_KV_HEADS, True)], axis=1)
    n_blk = A_KV_HEADS * LANES
    scale = HEAD_DIM ** -0.5
    gain = jnp.concatenate([jnp.tile(p["q_gain"], A_HEADS) * (scale * LOG2E), jnp.tile(p["k_gain"], 2 * A_KV_HEADS),
                            jnp.ones((2 * n_blk,), F32)]).reshape(1, -1)
    colbias = jnp.concatenate([jnp.zeros((nq + n_blk,), F32), _ones_row(A_KV_HEADS, False),
                               _ones_row(A_KV_HEADS, True)]).reshape(1, -1)
    seg = _segment_matrix([(0, HEAD_DIM), (HEAD_DIM, LANES)])
    tn = 512
    qkv = _proj(L, "gqa_qkv", x, w_p, tn=tn, mod=mod, g=g_mix, seg=seg, gain=gain, colbias=colbias,
                n_norm=(nq + n_blk) // tn, rope=(tables["gqa_cos"], tables["gqa_sin"], HEAD_DIM // 4))
    q_blocks = nq // LANES
    k_blocks = n_blk // LANES
    rep_pairs = N_PAIRS // A_KV_HEADS
    bound = _logit_bound(_gain_sq(p["q_gain"]), _gain_sq(p["k_gain"]), scale)
    return _attention(L, "gqa_attn", bound, qkv, [lambda p_: p_], qkv, [lambda p_: q_blocks + p_ // rep_pairs],
                      qkv, [lambda p_: q_blocks + k_blocks + p_ // rep_pairs,
                            lambda p_: q_blocks + 2 * k_blocks + p_ // rep_pairs],
                      pair=True, mode="all" if need_ctx else "lat")


def _natten_mixer(L, x, mod, g_mix, p, need_ctx):
    n = B_HEADS * HEAD_DIM
    w = p["w_qkv"]
    w_v = w[:, 2 * n:]
    w_p = jnp.concatenate([w[:, :2 * n], _alternate_heads(_half_pad_heads(w_v, B_HEADS, False),
                                                          _half_pad_heads(w_v, B_HEADS, True), B_HEADS)], axis=1)
    scale = HEAD_DIM ** -0.5
    n_v = B_HEADS * LANES
    gain = jnp.concatenate([jnp.tile(p["q_gain"], B_HEADS) * (scale * LOG2E), jnp.tile(p["k_gain"], B_HEADS),
                            jnp.ones((n_v,), F32)]).reshape(1, -1)
    colbias = jnp.concatenate([jnp.zeros((2 * n,), F32),
                               _alternate_heads(_ones_row(B_HEADS, False), _ones_row(B_HEADS, True), B_HEADS)]
                              ).reshape(1, -1)
    seg = _segment_matrix([(0, HEAD_DIM), (HEAD_DIM, LANES)])
    tn = 512
    qkv = _proj(L, "nat_qkv", x, w_p, tn=tn, mod=mod, g=g_mix, seg=seg, gain=gain, colbias=colbias,
                n_norm=2 * n // tn)
    qk_bound = _logit_bound(_gain_sq(p["q_gain"]), _gain_sq(p["k_gain"]), scale)
    bound = qk_bound + jnp.maximum(jnp.max(p["rel_bias"]), 0.0) * LOG2E
    o = _natten(L, bound, qkv, _natten_bias(p["rel_bias"]))
    if need_ctx:
        o_ctx = _attention(L, "nat_ctx_attn", qk_bound, qkv, [lambda p_: p_], qkv, [lambda p_: N_PAIRS + p_],
                           qkv, [lambda p_: 2 * N_PAIRS + 2 * p_, lambda p_: 2 * N_PAIRS + 2 * p_ + 1],
                           pair=True, mode="ctx")
        o = jnp.concatenate([o, o_ctx], axis=0)
    return o


def _mla_mixer(L, x, mod, g_mix, p, need_ctx, tables):
    d = L.D
    w_in = p["w_in"]
    n_c = Q_LORA + KV_LORA
    w_in_p = jnp.concatenate([w_in[:, :n_c], jnp.zeros((d, NOPE_DIM), F32), w_in[:, n_c:],
                              jnp.zeros((d, LANES - NOPE_DIM - ROPE_DIM), F32)], axis=1)
    dq = NOPE_DIM + ROPE_DIM
    pad_q = jnp.zeros((LANES - dq,), F32)
    seg_rope = _segment_matrix([(NOPE_DIM, dq)])
    gain_kr = jnp.concatenate([jnp.zeros((NOPE_DIM,), F32), p["k_gain"][NOPE_DIM:], pad_q]).reshape(1, LANES)
    cos, sin = tables["mla_cos"], tables["mla_sin"]
    cq, ckv, kr = _mla_in(L, x, mod, g_mix, w_in_p, p["g_dq"], p["g_dkv"], seg_rope, gain_kr, cos, sin)

    scale = dq ** -0.5
    w_uq = jnp.pad(p["w_uq"].reshape(Q_LORA, C_HEADS, dq), ((0, 0), (0, 0), (0, LANES - dq)))
    gain_q = jnp.tile(jnp.concatenate([p["q_gain"], pad_q]) * (scale * LOG2E), C_HEADS).reshape(1, -1)
    seg_q = _segment_matrix([(0, NOPE_DIM), (NOPE_DIM, dq)])
    n_hl = C_HEADS * LANES
    tn = 512
    q = _proj(L, "mla_q", cq, w_uq.reshape(Q_LORA, n_hl), tn=tn, seg=seg_q, gain=gain_q, n_norm=n_hl // tn,
              rope=(cos, sin, ROPE_DIM // 4))

    w_ukv = p["w_ukv"].reshape(KV_LORA, C_HEADS, NOPE_DIM + V_DIM)
    w_uk = jnp.pad(w_ukv[:, :, :NOPE_DIM], ((0, 0), (0, 0), (0, LANES - NOPE_DIM))).reshape(KV_LORA, n_hl)
    w_v = w_ukv[:, :, NOPE_DIM:].reshape(KV_LORA, C_HEADS * V_DIM)
    w_uv = _alternate_heads(_half_pad_heads(w_v, C_HEADS, False), _half_pad_heads(w_v, C_HEADS, True), C_HEADS)
    gain_k = jnp.tile(jnp.concatenate([p["k_gain"][:NOPE_DIM], jnp.zeros((LANES - NOPE_DIM,), F32)]),
                      C_HEADS).reshape(1, -1)
    seg_k = _segment_matrix([(0, NOPE_DIM)])
    k = _proj(L, "mla_k", ckv, w_uk, tn=tn, seg=seg_k, gain=gain_k, n_norm=n_hl // tn, add=kr)
    ones = _alternate_heads(_ones_row(C_HEADS, False), _ones_row(C_HEADS, True), C_HEADS).reshape(1, -1)
    v = _proj(L, "mla_v", ckv, w_uv, tn=tn, colbias=ones)
    q_sq = _gain_sq(p["q_gain"][:NOPE_DIM]) + _gain_sq(p["q_gain"][NOPE_DIM:])
    k_sq = _gain_sq(p["k_gain"][:NOPE_DIM]) + _gain_sq(p["k_gain"][NOPE_DIM:])
    bound = _logit_bound(q_sq, k_sq, scale)
    heads = [lambda p_: 2 * p_, lambda p_: 2 * p_ + 1]
    return _attention(L, "mla_attn", bound, q, heads, k, heads, v, heads, pair=False,
                      mode="all" if need_ctx else "lat")


def kernel(x, c, ctx, c_ctx, l0_w_mod, l0_b_mod, l0_g_mix, l0_g_ffn, l0_w_qkv, l0_q_gain, l0_k_gain, l0_w_o, l0_w_gu, l0_w_down, l1_w_mod, l1_b_mod, l1_g_mix, l1_g_ffn, l1_w_qkv, l1_q_gain, l1_k_gain, l1_rel_bias, l1_w_o, l1_w_router, l1_we_gu, l1_we_down, l2_w_mod, l2_b_mod, l2_g_mix, l2_g_ffn, l2_w_in, l2_g_dq, l2_g_dkv, l2_w_uq, l2_w_ukv, l2_q_gain, l2_k_gain, l2_w_o, l2_w_gu, l2_w_down, l3_w_mod, l3_b_mod, l3_g_mix, l3_g_ffn, l3_w_qkv, l3_q_gain, l3_k_gain, l3_w_o, l3_w_router, l3_we_gu, l3_we_down):
    layers = [
        dict(w_mod=l0_w_mod, b_mod=l0_b_mod, g_mix=l0_g_mix, g_ffn=l0_g_ffn, kind="gqa",
             mix=dict(w_qkv=l0_w_qkv, q_gain=l0_q_gain, k_gain=l0_k_gain), w_o=l0_w_o,
             ffn=dict(w_gu=l0_w_gu, w_down=l0_w_down)),
        dict(w_mod=l1_w_mod, b_mod=l1_b_mod, g_mix=l1_g_mix, g_ffn=l1_g_ffn, kind="natten",
             mix=dict(w_qkv=l1_w_qkv, q_gain=l1_q_gain, k_gain=l1_k_gain, rel_bias=l1_rel_bias), w_o=l1_w_o,
             moe=dict(w_router=l1_w_router, we_gu=l1_we_gu, we_down=l1_we_down)),
        dict(w_mod=l2_w_mod, b_mod=l2_b_mod, g_mix=l2_g_mix, g_ffn=l2_g_ffn, kind="mla",
             mix=dict(w_in=l2_w_in, g_dq=l2_g_dq, g_dkv=l2_g_dkv, w_uq=l2_w_uq, w_ukv=l2_w_ukv,
                      q_gain=l2_q_gain, k_gain=l2_k_gain), w_o=l2_w_o,
             ffn=dict(w_gu=l2_w_gu, w_down=l2_w_down)),
        dict(w_mod=l3_w_mod, b_mod=l3_b_mod, g_mix=l3_g_mix, g_ffn=l3_g_ffn, kind="gqa",
             mix=dict(w_qkv=l3_w_qkv, q_gain=l3_q_gain, k_gain=l3_k_gain), w_o=l3_w_o,
             moe=dict(w_router=l3_w_router, we_gu=l3_we_gu, we_down=l3_we_down)),
    ]
    batch, seq, d_model = x.shape
    L = _Layout(batch, seq, ctx.shape[1], d_model)
    gqa_cos, gqa_sin = _gqa_rope_tables(L)
    mla_cos, mla_sin = _mla_rope_tables(L)
    tables = dict(gqa_cos=gqa_cos, gqa_sin=gqa_sin, mla_cos=mla_cos, mla_sin=mla_sin)
    cvec = jnp.concatenate([c, c_ctx[None, :], jnp.zeros((SUBLANES - batch - 1, d_model), F32)], axis=0)
    xs = jnp.concatenate([x.reshape(L.NL, d_model), ctx.reshape(L.NC, d_model)], axis=0)
    for li, p in enumerate(layers):
        need_ctx = li < len(layers) - 1
        mod = _adaln(cvec, p["w_mod"], p["b_mod"], batch + 1)
        if p["kind"] == "gqa":
            o = _gqa_mixer(L, xs, mod, p["g_mix"], p["mix"], need_ctx, tables)
        elif p["kind"] == "natten":
            o = _natten_mixer(L, xs, mod, p["g_mix"], p["mix"], need_ctx)
        else:
            o = _mla_mixer(L, xs, mod, p["g_mix"], p["mix"], need_ctx, tables)
        xs = _oproj(L, o, p["w_o"], xs, mod, need_ctx)
        if "ffn" in p:
            xs = _ffn(L, xs, mod, p["g_ffn"], p["ffn"]["w_gu"], p["ffn"]["w_down"], need_ctx)
        else:
            xs = _moe(L, xs, mod, p["g_ffn"], p["moe"]["w_router"], p["moe"]["we_gu"], p["moe"]["we_down"], need_ctx)
    return xs[:L.NL].reshape(batch, seq, d_model)
```

```python
import functools
import math

import jax
import jax.numpy as jnp
from jax import lax
from jax.experimental import pallas as pl
from jax.experimental.pallas import tpu as pltpu

F32 = jnp.float32
BF16 = jnp.bfloat16

LANES = 128
SUBLANES = 8
VMEM_LIMIT_BYTES = 56 * 1024 * 1024

NORM_EPS = 1e-6
ROPE_THETA = 10000.0
NEG_INF = -1e30
GRID_W = 64
HEAD_DIM = 64
WIN_R = 8
WIN_C = 16
A_HEADS = 16
A_KV_HEADS = 4
B_HEADS = 16
C_HEADS = 16
Q_LORA = 384
KV_LORA = 256
NOPE_DIM = 64
ROPE_DIM = 32
V_DIM = 64
N_EXPERTS = 8
N_PAIRS = 8
LOG2E = math.log2(math.e)
FAST_BOUND = 40.0
BOUND_MARGIN = 1.02
ONES_LANE = (HEAD_DIM, 0)
EXPERT_ROW_STEPS = 4
NAT_ROWS = 4
NAT_SPAN = NAT_ROWS + WIN_R

SH1, SC1, G1, SH2, SC2, G2 = range(6)


def _params(sem):
    return pltpu.CompilerParams(dimension_semantics=sem, vmem_limit_bytes=VMEM_LIMIT_BYTES)


def _dot(a, b):
    return jnp.dot(a, b, preferred_element_type=F32)


def _dot_nt(a, b):
    return lax.dot_general(a, b, (((1,), (1,)), ((), ())), preferred_element_type=F32)


def _dot3(a, b):
    a_hi, b_hi = a.astype(BF16), b.astype(BF16)
    a_lo, b_lo = (a - a_hi.astype(F32)).astype(BF16), (b - b_hi.astype(F32)).astype(BF16)
    return _dot(a_hi, b_hi) + (_dot(a_lo, b_hi) + _dot(a_hi, b_lo))


def _silu(x):
    return x * (1.0 / (1.0 + jnp.exp(-x)))


def _norm_mod(x, g, shift, scale):
    ms = jnp.mean(x * x, axis=-1, keepdims=True)
    y = x * lax.rsqrt(ms + NORM_EPS) * g
    return y * (1.0 + scale) + shift


def _seg_mean_sq(y, seg):
    return _dot((y * y).astype(BF16), seg)


def _rope(y, cos, sin_signed, shift):
    if 2 * shift == LANES:
        return y * cos + pltpu.roll(y, shift, 1) * sin_signed
    lane = lax.broadcasted_iota(jnp.int32, (1, y.shape[1]), 1)
    first = (lane & (2 * shift - 1)) < shift
    partner = jnp.where(first, pltpu.roll(y, LANES - shift, 1), pltpu.roll(y, shift, 1))
    return y * cos + partner * sin_signed


class _Layout:
    def __init__(self, batch, seq, ctx_len, d_model):
        self.B, self.T, self.TC, self.D = batch, seq, ctx_len, d_model
        self.NL = batch * seq
        self.NC = batch * ctx_len
        self.N = self.NL + self.NC
        self.tm = 1024 if (self.NC % 1024 == 0 and seq % 1024 == 0) else ctx_len
        assert seq % self.tm == 0 and self.NC % self.tm == 0
        self.tpb = seq // self.tm
        self.tq = ctx_len
        assert seq % self.tq == 0 and seq % GRID_W == 0

    def mod_group(self, i):
        return jnp.minimum(i // self.tpb, self.B)

    def rope_block(self, i):
        return jnp.where(i < self.B * self.tpb, i % self.tpb, self.tpb)

    def rows(self, with_ctx):
        return self.N if with_ctx else self.NL

    def tiles(self, with_ctx):
        return self.rows(with_ctx) // self.tm


def _adaln_kernel(c_ref, w_ref, b_ref, o_ref):
    s = _silu(c_ref[...])
    o_ref[...] = _dot3(s, w_ref[...]) + b_ref[...]


def _adaln(cvec, w_mod, b_mod, n_groups):
    d, n_out = w_mod.shape
    tn = n_out // 4
    out = pl.pallas_call(
        _adaln_kernel,
        out_shape=jax.ShapeDtypeStruct((SUBLANES, n_out), F32),
        grid=(n_out // tn,),
        in_specs=[pl.BlockSpec((SUBLANES, d), lambda j: (0, 0)),
                  pl.BlockSpec((d, tn), lambda j: (0, j)),
                  pl.BlockSpec((1, tn), lambda j: (0, j))],
        out_specs=pl.BlockSpec((SUBLANES, tn), lambda j: (0, j)),
        compiler_params=_params(("arbitrary",)),
        name="adaln",
    )(cvec, w_mod, b_mod.reshape(1, n_out))
    mod = out[:n_groups].reshape(n_groups, 6, d)
    return jnp.pad(mod, ((0, 0), (0, 2), (0, 0)))


def _proj_kernel(*refs, n_x, lat_tiles, has_mod, n_norm, n_col_tiles, rope_shift, has_add, has_colbias, groups):
    refs = list(refs)
    x_refs = [refs.pop(0) for _ in range(n_x)]
    x_ref = x_refs[0]
    if has_mod:
        mod_ref, g_ref = refs.pop(0), refs.pop(0)
    w_ref = refs.pop(0)
    if n_norm:
        seg_ref, gain_ref = refs.pop(0), refs.pop(0)
    if rope_shift:
        cos_ref, sin_ref = refs.pop(0), refs.pop(0)
    if has_add:
        add_ref = refs.pop(0)
    if has_colbias:
        colbias_ref = refs.pop(0)
    o_ref = refs.pop(0)
    j = pl.program_id(1)

    if has_mod:
        h_scr = refs.pop(0)

        def fill(part):
            h = _norm_mod(x_refs[part][...], g_ref[...], mod_ref[SH1:SH1 + 1, :], mod_ref[SC1:SC1 + 1, :])
            h_scr[...] = h.astype(BF16)

        pl.when(j == 0)(functools.partial(_per_part, n_x, lat_tiles, fill))
        lhs_ref = h_scr
    else:
        lhs_ref = x_ref

    def matmul():
        return _dot(lhs_ref[...], w_ref[...].astype(BF16))

    def normed():
        acc = matmul()
        for c in range(groups):
            sl = slice(c * LANES, (c + 1) * LANES)
            y = acc[:, sl]
            y = y * lax.rsqrt(_seg_mean_sq(y, seg_ref[...]) + NORM_EPS) * gain_ref[:, sl]
            if rope_shift:
                y = _rope(y, cos_ref[...], sin_ref[...], rope_shift)
            if has_add:
                y = y + add_ref[...]
            o_ref[:, sl] = y.astype(o_ref.dtype)

    def plain():
        acc = matmul()
        y = acc + colbias_ref[...] if has_colbias else acc
        o_ref[...] = y.astype(o_ref.dtype)

    if n_norm == 0:
        plain()
    elif n_norm == n_col_tiles:
        normed()
    else:
        pl.when(j < n_norm)(normed)
        pl.when(j >= n_norm)(plain)


def _proj(L, name, x, w, *, with_ctx=True, tn=512, mod=None, g=None, seg=None, gain=None, n_norm=0,
          rope=None, add=None, colbias=None, out_dtype=BF16):
    k, n_out = w.shape
    tm = L.tm
    n_col = n_out // tn
    x = _as_parts(x)
    assert len(x) == 1 or mod is not None
    lat_tiles = L.NL // tm
    in_specs = _row_specs(x, tm, lat_tiles)
    args = list(x)
    scratch = []
    if mod is not None:
        in_specs += [pl.BlockSpec((None, SUBLANES, k), lambda i, j: (L.mod_group(i), 0, 0)),
                     pl.BlockSpec((1, k), lambda i, j: (0, 0))]
        args += [mod, g.reshape(1, k)]
        scratch.append(pltpu.VMEM((tm, k), BF16))
    in_specs.append(pl.BlockSpec((k, tn), lambda i, j: (0, j)))
    args.append(w)
    if n_norm:
        in_specs += [pl.BlockSpec((LANES, LANES), lambda i, j: (0, 0)),
                     pl.BlockSpec((1, tn), lambda i, j: (0, j))]
        args += [seg, gain]
    rope_shift = 0
    if rope is not None:
        cos, sin, rope_shift = rope
        in_specs += [pl.BlockSpec((tm, LANES), lambda i, j: (L.rope_block(i), 0))] * 2
        args += [cos, sin]
    if add is not None:
        in_specs.append(pl.BlockSpec((tm, LANES), lambda i, j: (i, 0)))
        args.append(add)
    if colbias is not None:
        in_specs.append(pl.BlockSpec((1, tn), lambda i, j: (0, j)))
        args.append(colbias)
    kern = functools.partial(_proj_kernel, n_x=len(x), lat_tiles=lat_tiles, has_mod=mod is not None, n_norm=n_norm, n_col_tiles=n_col,
                             rope_shift=rope_shift, has_add=add is not None, has_colbias=colbias is not None,
                             groups=tn // LANES)
    return pl.pallas_call(
        kern,
        out_shape=jax.ShapeDtypeStruct((L.N, n_out), out_dtype),
        grid=(L.tiles(with_ctx), n_col),
        in_specs=in_specs,
        out_specs=pl.BlockSpec((tm, tn), lambda i, j: (i, j)),
        scratch_shapes=scratch,
        compiler_params=_params(("parallel", "arbitrary")),
        name=name,
    )(*args)


def _mla_in_kernel(x_ref, mod_ref, g_ref, w_ref, gdq_ref, gdkv_ref, seg_ref, gain_ref, cos_ref, sin_ref,
                   cq_ref, ckv_ref, kr_ref):
    h = _norm_mod(x_ref[...], g_ref[...], mod_ref[SH1:SH1 + 1, :], mod_ref[SC1:SC1 + 1, :])
    acc = _dot(h.astype(BF16), w_ref[...].astype(BF16))
    cq = acc[:, :Q_LORA]
    cq = cq * lax.rsqrt(jnp.mean(cq * cq, axis=-1, keepdims=True) + NORM_EPS) * gdq_ref[...]
    cq_ref[...] = cq.astype(BF16)
    ckv = acc[:, Q_LORA:Q_LORA + KV_LORA]
    ckv = ckv * lax.rsqrt(jnp.mean(ckv * ckv, axis=-1, keepdims=True) + NORM_EPS) * gdkv_ref[...]
    ckv_ref[...] = ckv.astype(BF16)
    kr = acc[:, Q_LORA + KV_LORA:]
    kr = kr * lax.rsqrt(_seg_mean_sq(kr, seg_ref[...]) + NORM_EPS) * gain_ref[...]
    kr_ref[...] = _rope(kr, cos_ref[...], sin_ref[...], ROPE_DIM // 4)


def _mla_in(L, x, mod, g, w_in_p, g_dq, g_dkv, seg_kr, gain_kr, cos, sin):
    tm, d = L.tm, L.D
    n_out = w_in_p.shape[1]
    full = lambda shape: pl.BlockSpec(shape, lambda i: (0,) * len(shape))
    return pl.pallas_call(
        _mla_in_kernel,
        out_shape=(jax.ShapeDtypeStruct((L.N, Q_LORA), BF16),
                   jax.ShapeDtypeStruct((L.N, KV_LORA), BF16),
                   jax.ShapeDtypeStruct((L.N, LANES), F32)),
        grid=(L.tiles(True),),
        in_specs=[pl.BlockSpec((tm, d), lambda i: (i, 0)),
                  pl.BlockSpec((None, SUBLANES, d), lambda i: (L.mod_group(i), 0, 0)),
                  full((1, d)), full((d, n_out)), full((1, Q_LORA)), full((1, KV_LORA)),
                  full((LANES, LANES)), full((1, LANES)),
                  pl.BlockSpec((tm, LANES), lambda i: (L.rope_block(i), 0)),
                  pl.BlockSpec((tm, LANES), lambda i: (L.rope_block(i), 0))],
        out_specs=(pl.BlockSpec((tm, Q_LORA), lambda i: (i, 0)),
                   pl.BlockSpec((tm, KV_LORA), lambda i: (i, 0)),
                   pl.BlockSpec((tm, LANES), lambda i: (i, 0))),
        compiler_params=_params(("parallel",)),
        name="mla_in",
    )(x, mod, g.reshape(1, d), w_in_p, g_dq.reshape(1, -1), g_dkv.reshape(1, -1), seg_kr, gain_kr, cos, sin)


def _softmax_pv(s_list, v_list, bound, ones_lane, fast):
    if fast:
        o = None
        for s, v in zip(s_list, v_list):
            part = _dot(jnp.exp2(s - bound).astype(BF16), v)
            o = part if o is None else o + part
        return o / o[:, ones_lane:ones_lane + 1]
    m = None
    for s in s_list:
        ms = jnp.max(s, axis=-1, keepdims=True)
        m = ms if m is None else jnp.maximum(m, ms)
    o, den = None, None
    for s, v in zip(s_list, v_list):
        p = jnp.exp2(s - m)
        ds = jnp.sum(p, axis=-1, keepdims=True)
        part = _dot(p.astype(BF16), v)
        o = part if o is None else o + part
        den = ds if den is None else den + ds
    return o / den


def _attn_kernel(*refs, nq, nk, pair, q_head_lanes, lat_keys, nq_lat, ctx_q):
    refs = list(refs)
    bound_ref = refs.pop(0)
    q_refs = [refs.pop(0) for _ in range(nq)]
    kl_refs = [refs.pop(0) for _ in range(nk)] if lat_keys else []
    kc_refs = [refs.pop(0) for _ in range(nk)]
    vl_refs = [refs.pop(0) for _ in range(2)] if lat_keys else []
    vc_refs = [refs.pop(0) for _ in range(2)]
    o_ref = refs.pop(0)
    lane = lax.broadcasted_iota(jnp.int32, (1, LANES), 1)
    lo = lane < HEAD_DIM
    q_first = (lane & (2 * q_head_lanes - 1)) < q_head_lanes
    bound = bound_ref[0, 0]

    def run(use_lat, fast):
        outs = []
        for hh in range(2):
            q = q_refs[hh % nq][...]
            if pair:
                q = jnp.where(q_first if hh == 0 else jnp.logical_not(q_first), q, jnp.zeros_like(q))
            s_list = [_dot_nt(q, kc_refs[hh % nk][...])]
            v_list = [vc_refs[hh][...]]
            if use_lat:
                s_list.append(_dot_nt(q, kl_refs[hh % nk][...]))
                v_list.append(vl_refs[hh][...])
            outs.append(_softmax_pv(s_list, v_list, bound, ONES_LANE[hh], fast))
        o_ref[...] = jnp.where(lo, outs[0], outs[1]).astype(o_ref.dtype)

    is_fast = bound <= FAST_BOUND
    for fast in (True, False):
        pred = is_fast if fast else jnp.logical_not(is_fast)
        if lat_keys and ctx_q:
            i = pl.program_id(2)
            pl.when(jnp.logical_and(pred, i < nq_lat))(functools.partial(run, True, fast))
            pl.when(jnp.logical_and(pred, i == nq_lat))(functools.partial(run, False, fast))
        else:
            pl.when(pred)(functools.partial(run, lat_keys, fast))


def _attention(L, name, bound, q, q_cols, k, k_cols, v, v_cols, *, pair, mode, q_head_lanes=HEAD_DIM):
    B, T, TC = L.B, L.T, L.TC
    tq = 2 * L.tq if (mode == "lat" and T % (2 * L.tq) == 0) else L.tq
    nq_lat = T // tq
    ctx_row = L.NL // TC
    lat_keys = mode != "ctx"
    ctx_q = mode != "lat"
    n_i = {"all": nq_lat + 1, "lat": nq_lat, "ctx": 1}[mode]

    def q_row(b, i):
        if mode == "ctx":
            return ctx_row + b
        if mode == "lat":
            return b * nq_lat + i
        return jnp.where(i < nq_lat, b * nq_lat + i, ctx_row + b)

    in_specs = [pl.BlockSpec(memory_space=pltpu.SMEM)]
    args = [bound]
    for f in q_cols:
        in_specs.append(pl.BlockSpec((tq, LANES), lambda b, p, i, f=f: (q_row(b, i), f(p))))
        args.append(q)
    if lat_keys:
        for f in k_cols:
            in_specs.append(pl.BlockSpec((T, LANES), lambda b, p, i, f=f: (b, f(p))))
            args.append(k)
    for f in k_cols:
        in_specs.append(pl.BlockSpec((TC, LANES), lambda b, p, i, f=f: (ctx_row + b, f(p))))
        args.append(k)
    if lat_keys:
        for f in v_cols:
            in_specs.append(pl.BlockSpec((T, LANES), lambda b, p, i, f=f: (b, f(p))))
            args.append(v)
    for f in v_cols:
        in_specs.append(pl.BlockSpec((TC, LANES), lambda b, p, i, f=f: (ctx_row + b, f(p))))
        args.append(v)
    out_rows = {"all": L.N, "lat": L.NL, "ctx": L.NC}[mode]
    out_row = (lambda b, i: b) if mode == "ctx" else q_row
    kern = functools.partial(_attn_kernel, nq=len(q_cols), nk=len(k_cols), pair=pair, q_head_lanes=q_head_lanes,
                             lat_keys=lat_keys, nq_lat=nq_lat, ctx_q=ctx_q)
    return pl.pallas_call(
        kern,
        out_shape=jax.ShapeDtypeStruct((out_rows, N_PAIRS * LANES), BF16),
        grid=(B, N_PAIRS, n_i),
        in_specs=in_specs,
        out_specs=pl.BlockSpec((tq, LANES), lambda b, p, i: (out_row(b, i), p)),
        compiler_params=_params(("parallel", "parallel", "arbitrary")),
        name=name,
    )(*args)


def _natten_kernel(bound_ref, q_ref, k_ref, va_ref, vb_ref, kc_ref, vca_ref, vcb_ref, bias_ref, o_ref, oc_scr, *,
                   rows_n):
    lo = lax.broadcasted_iota(jnp.int32, (1, LANES), 1) < HEAD_DIM
    halves = (lo, jnp.logical_not(lo))
    v_refs = (va_ref, vb_ref)
    vc_refs = (vca_ref, vcb_ref)
    n_groups = rows_n // NAT_ROWS
    bound = bound_ref[0, 0]

    def group(g):
        u0 = jnp.clip(g * NAT_ROWS - WIN_R // 2, 0, rows_n - NAT_SPAN)
        kind = jnp.where(g == 0, 0, jnp.where(g == n_groups - 1, 2, 1))
        rows = pl.ds(pl.multiple_of(g * (NAT_ROWS * GRID_W), NAT_ROWS * GRID_W), NAT_ROWS * GRID_W)
        keys = pl.ds(pl.multiple_of(u0 * GRID_W, GRID_W), NAT_SPAN * GRID_W)
        return kind, rows, keys

    def masked(q, hh):
        return jnp.where(halves[hh], q, jnp.zeros_like(q))

    def fast_body(g, carry):
        kind, rows, keys = group(g)
        q = q_ref[rows, :]
        kw = k_ref[keys, :]
        outs = []
        for hh in range(2):
            s_l = _dot_nt(masked(q, hh), kw) + bias_ref[hh, kind]
            o = oc_scr[hh, rows, :] + _dot(jnp.exp2(s_l - bound).astype(BF16), v_refs[hh][keys, :])
            outs.append(o / o[:, ONES_LANE[hh]:ONES_LANE[hh] + 1])
        o_ref[rows, :] = jnp.where(lo, outs[0], outs[1]).astype(o_ref.dtype)
        return carry

    def exact_body(g, carry):
        kind, rows, keys = group(g)
        q = q_ref[rows, :]
        kw = k_ref[keys, :]
        outs = []
        for hh in range(2):
            qh = masked(q, hh)
            s_l = _dot_nt(qh, kw) + bias_ref[hh, kind]
            s_c = _dot_nt(qh, kc_ref[...])
            outs.append(_softmax_pv([s_l, s_c], [v_refs[hh][keys, :], vc_refs[hh][...]], bound, ONES_LANE[hh], False))
        o_ref[rows, :] = jnp.where(lo, outs[0], outs[1]).astype(o_ref.dtype)
        return carry

    is_fast = bound <= FAST_BOUND

    @pl.when(is_fast)
    def _():
        q_all = q_ref[...]
        for hh in range(2):
            p_c = jnp.exp2(_dot_nt(masked(q_all, hh), kc_ref[...]) - bound).astype(BF16)
            oc_scr[hh] = _dot(p_c, vc_refs[hh][...])
        lax.fori_loop(0, n_groups, fast_body, 0, unroll=2)

    @pl.when(jnp.logical_not(is_fast))
    def _():
        lax.fori_loop(0, n_groups, exact_body, 0)


def _natten(L, bound, qkv, bias):
    B, T, TC = L.B, L.T, L.TC
    ctx_row = L.NL // TC
    rows_n = T // GRID_W
    assert rows_n % NAT_ROWS == 0 and rows_n >= NAT_SPAN and NAT_ROWS == WIN_R // 2
    k0, v0 = N_PAIRS, 2 * N_PAIRS
    return pl.pallas_call(
        functools.partial(_natten_kernel, rows_n=T // GRID_W),
        out_shape=jax.ShapeDtypeStruct((L.NL, N_PAIRS * LANES), BF16),
        grid=(B, N_PAIRS),
        in_specs=[pl.BlockSpec(memory_space=pltpu.SMEM),
                  pl.BlockSpec((T, LANES), lambda b, p: (b, p)),
                  pl.BlockSpec((T, LANES), lambda b, p: (b, k0 + p)),
                  pl.BlockSpec((T, LANES), lambda b, p: (b, v0 + 2 * p)),
                  pl.BlockSpec((T, LANES), lambda b, p: (b, v0 + 2 * p + 1)),
                  pl.BlockSpec((TC, LANES), lambda b, p: (ctx_row + b, k0 + p)),
                  pl.BlockSpec((TC, LANES), lambda b, p: (ctx_row + b, v0 + 2 * p)),
                  pl.BlockSpec((TC, LANES), lambda b, p: (ctx_row + b, v0 + 2 * p + 1)),
                  pl.BlockSpec((2, 3, NAT_ROWS * GRID_W, NAT_SPAN * GRID_W), lambda b, p: (p, 0, 0, 0))],
        out_specs=pl.BlockSpec((T, LANES), lambda b, p: (b, p)),
        scratch_shapes=[pltpu.VMEM((2, T, LANES), F32)],
        compiler_params=_params(("parallel", "parallel")),
        name="natten",
    )(bound, qkv, qkv, qkv, qkv, qkv, qkv, qkv, bias)


def _row_specs(arrs, tm, lat_tiles):
    if len(arrs) == 1:
        return [pl.BlockSpec((tm, arrs[0].shape[1]), lambda i, *_: (i, 0))]
    lat, ctx = arrs
    return [pl.BlockSpec((tm, lat.shape[1]), lambda i, *_: (jnp.minimum(i, lat_tiles - 1), 0)),
            pl.BlockSpec((tm, ctx.shape[1]), lambda i, *_: (jnp.maximum(i - lat_tiles, 0), 0))]


def _per_part(n_parts, lat_tiles, fn):
    if n_parts == 1:
        fn(0)
    else:
        is_lat = pl.program_id(0) < lat_tiles
        pl.when(is_lat)(functools.partial(fn, 0))
        pl.when(jnp.logical_not(is_lat))(functools.partial(fn, 1))


def _as_parts(a):
    return list(a) if isinstance(a, (tuple, list)) else [a]


def _oproj_kernel(*refs, n_o, n_x, lat_tiles):
    refs = list(refs)
    o_refs = [refs.pop(0) for _ in range(n_o)]
    w_ref = refs.pop(0)
    x_refs = [refs.pop(0) for _ in range(n_x)]
    mod_ref, out_ref = refs

    def run(part):
        acc = _dot(o_refs[min(part, n_o - 1)][...], w_ref[...].astype(BF16))
        out_ref[...] = x_refs[min(part, n_x - 1)][...] + mod_ref[G1:G1 + 1, :] * acc

    _per_part(max(n_o, n_x), lat_tiles, run)


def _oproj(L, o, w_o, x, mod, with_ctx):
    tm, d = L.tm // 2, L.D
    k = w_o.shape[0]
    ratio = L.tm // tm
    o, x = _as_parts(o), _as_parts(x)
    lat_tiles = L.NL // tm
    return pl.pallas_call(
        functools.partial(_oproj_kernel, n_o=len(o), n_x=len(x), lat_tiles=lat_tiles),
        out_shape=jax.ShapeDtypeStruct((L.rows(with_ctx), d), F32),
        grid=(L.tiles(with_ctx) * ratio,),
        in_specs=_row_specs(o, tm, lat_tiles) + [pl.BlockSpec((k, d), lambda i: (0, 0))]
        + _row_specs(x, tm, lat_tiles)
        + [pl.BlockSpec((None, SUBLANES, d), lambda i: (L.mod_group(i // ratio), 0, 0))],
        out_specs=pl.BlockSpec((tm, d), lambda i: (i, 0)),
        compiler_params=_params(("parallel",)),
        name="oproj",
    )(*o, w_o, *x, mod)


def _ffn_kernel(x_ref, mod_ref, g_ref, wgu_ref, wd_ref, out_ref, h_scr, acc_scr, *, d_ff, tf):
    h = _norm_mod(x_ref[...], g_ref[...], mod_ref[SH2:SH2 + 1, :], mod_ref[SC2:SC2 + 1, :])
    h_scr[...] = h.astype(BF16)
    for c in range(d_ff // tf):
        h = h_scr[...]
        gate = _dot(h, wgu_ref[:, c * tf:(c + 1) * tf])
        up = _dot(h, wgu_ref[:, d_ff + c * tf:d_ff + (c + 1) * tf])
        act = (_silu(gate) * up).astype(BF16)
        part = _dot(act, wd_ref[c * tf:(c + 1) * tf, :])
        if c == 0:
            acc_scr[...] = part
        else:
            acc_scr[...] += part
    out_ref[...] = x_ref[...] + mod_ref[G2:G2 + 1, :] * acc_scr[...]


def _ffn(L, x, mod, g, w_gu, w_down, with_ctx):
    tm, d = L.tm, L.D
    d_ff = w_down.shape[0]
    resident = lambda shape: pl.BlockSpec(shape, lambda i: (0, 0), pipeline_mode=pl.Buffered(1))
    return pl.pallas_call(
        functools.partial(_ffn_kernel, d_ff=d_ff, tf=256),
        out_shape=jax.ShapeDtypeStruct((L.rows(with_ctx), d), F32),
        grid=(L.tiles(with_ctx),),
        in_specs=[pl.BlockSpec((tm, d), lambda i: (i, 0)),
                  pl.BlockSpec((None, SUBLANES, d), lambda i: (L.mod_group(i), 0, 0)),
                  pl.BlockSpec((1, d), lambda i: (0, 0)),
                  resident((d, 2 * d_ff)),
                  resident((d_ff, d))],
        out_specs=pl.BlockSpec((tm, d), lambda i: (i, 0)),
        scratch_shapes=[pltpu.VMEM((tm, d), BF16), pltpu.VMEM((tm, d), F32)],
        compiler_params=_params(("parallel",)),
        name="ffn",
    )(x, mod, g.reshape(1, d), w_gu.astype(BF16), w_down.astype(BF16))


R_I1, R_I2, R_W1, R_W2, R_R1, R_R2 = range(6)


def _router_kernel(x_ref, mod_ref, g_ref, wr_ref, route_ref, cnt_ref, carry_scr, *, tm):
    i = pl.program_id(0)

    @pl.when(i == 0)
    def _():
        carry_scr[...] = jnp.zeros_like(carry_scr)

    h = _norm_mod(x_ref[...], g_ref[...], mod_ref[SH2:SH2 + 1, :], mod_ref[SC2:SC2 + 1, :])
    logits = _dot3(h, wr_ref[...])
    lane = lax.broadcasted_iota(jnp.int32, logits.shape, 1).astype(F32)
    lg = jnp.where(lane < N_EXPERTS, logits, -jnp.inf)
    m1 = jnp.max(lg, axis=-1, keepdims=True)
    i1 = jnp.min(jnp.where(lg == m1, lane, float(LANES)), axis=-1, keepdims=True)
    lg2 = jnp.where(lane == i1, -jnp.inf, lg)
    m2 = jnp.max(lg2, axis=-1, keepdims=True)
    i2 = jnp.min(jnp.where(lg2 == m2, lane, float(LANES)), axis=-1, keepdims=True)
    e = jnp.exp(m2 - m1)
    w1 = 1.0 / (1.0 + e)
    w2 = e / (1.0 + e)
    onehot = jnp.where((lane == i1) | (lane == i2), 1.0, 0.0)
    row = lax.broadcasted_iota(jnp.int32, (tm, tm), 0)
    col = lax.broadcasted_iota(jnp.int32, (tm, tm), 1)
    tri = jnp.where(row > col, 1.0, 0.0).astype(BF16)
    rank = _dot(tri, onehot.astype(BF16)) + carry_scr[0:1, :]
    r1 = jnp.sum(jnp.where(lane == i1, rank, 0.0), axis=-1, keepdims=True)
    r2 = jnp.sum(jnp.where(lane == i2, rank, 0.0), axis=-1, keepdims=True)
    carry_scr[...] = carry_scr[...] + jnp.sum(onehot, axis=0, keepdims=True)
    rec = jnp.zeros_like(logits)
    for pos, val in ((R_I1, i1), (R_I2, i2), (R_W1, w1), (R_W2, w2), (R_R1, r1), (R_R2, r2)):
        rec = jnp.where(lane == float(pos), val, rec)
    route_ref[...] = rec
    cnt_ref[...] = carry_scr[...]


def _router(L, x, mod, g, w_router, with_ctx):
    tm, d = L.tm, L.D
    wr = jnp.pad(w_router, ((0, 0), (0, LANES - w_router.shape[1])))
    return pl.pallas_call(
        functools.partial(_router_kernel, tm=tm),
        out_shape=(jax.ShapeDtypeStruct((L.rows(with_ctx), LANES), F32),
                   jax.ShapeDtypeStruct((SUBLANES, LANES), F32)),
        grid=(L.tiles(with_ctx),),
        in_specs=[pl.BlockSpec((tm, d), lambda i: (i, 0)),
                  pl.BlockSpec((None, SUBLANES, d), lambda i: (L.mod_group(i), 0, 0)),
                  pl.BlockSpec((1, d), lambda i: (0, 0)),
                  pl.BlockSpec((d, LANES), lambda i: (0, 0))],
        out_specs=(pl.BlockSpec((tm, LANES), lambda i: (i, 0)),
                   pl.BlockSpec((SUBLANES, LANES), lambda i: (0, 0))),
        scratch_shapes=[pltpu.VMEM((SUBLANES, LANES), F32)],
        compiler_params=_params(("arbitrary",)),
        name="router",
    )(x, mod, g.reshape(1, d), wr)


def _row_block(ref, row):
    return ref.at[pl.ds(pl.multiple_of(row * SUBLANES, SUBLANES), SUBLANES)]


def _dispatch_kernel(pos_ref, x_ref, mod_ref, g_ref, xs_in_ref, xs_ref, rows_scr, sem, *, tmd):
    del xs_in_ref
    h = _norm_mod(x_ref[...], g_ref[...], mod_ref[SH2:SH2 + 1, :], mod_ref[SC2:SC2 + 1, :])
    for c in range(h.shape[1] // LANES):
        rows_scr[pl.ds(c, tmd, stride=SUBLANES), :] = h[:, c * LANES:(c + 1) * LANES]

    def copies(t):
        src = _row_block(rows_scr, t)
        return (pltpu.make_async_copy(src, _row_block(xs_ref, pos_ref[0, 2 * t]), sem),
                pltpu.make_async_copy(src, _row_block(xs_ref, pos_ref[0, 2 * t + 1]), sem))

    def start(t, carry):
        for k, cp in enumerate(copies(t)):
            cp.start(priority=k)
        return carry

    def wait(t, carry):
        for cp in copies(t):
            cp.wait()
        return carry

    lax.fori_loop(0, tmd, start, 0)
    lax.fori_loop(0, tmd, wait, 0)


def _dispatch(L, x, mod, g, pos, n_slots, with_ctx):
    tmd, d = L.tm, L.D
    n_t = L.tiles(with_ctx)
    xs0 = jnp.zeros((n_slots * SUBLANES, LANES), F32)
    return pl.pallas_call(
        functools.partial(_dispatch_kernel, tmd=tmd),
        out_shape=jax.ShapeDtypeStruct(xs0.shape, F32),
        grid=(n_t,),
        in_specs=[pl.BlockSpec((None, 1, 2 * tmd), lambda i: (i, 0, 0), memory_space=pltpu.SMEM),
                  pl.BlockSpec((tmd, d), lambda i: (i, 0)),
                  pl.BlockSpec((None, SUBLANES, d), lambda i: (L.mod_group(i), 0, 0)),
                  pl.BlockSpec((1, d), lambda i: (0, 0)),
                  pl.BlockSpec(memory_space=pl.ANY)],
        out_specs=pl.BlockSpec(memory_space=pl.ANY),
        scratch_shapes=[pltpu.VMEM((tmd * SUBLANES, LANES), F32), pltpu.SemaphoreType.DMA(())],
        input_output_aliases={4: 0},
        compiler_params=pltpu.CompilerParams(dimension_semantics=("arbitrary",), has_side_effects=True,
                                             vmem_limit_bytes=VMEM_LIMIT_BYTES),
        name="moe_dispatch",
    )(pos.reshape(-1, 1, 2 * tmd)[:n_t], x, mod, g.reshape(1, d), xs0)


def _experts_kernel(te_ref, tr_ref, xs_ref, wg_ref, wu_ref, wd_ref, ys_ref, h_scr, acc_scr, *, tmx):
    del te_ref
    t, c = pl.program_id(0), pl.program_id(1)
    rows_valid = tr_ref[t]
    groups = h_scr.shape[1] // LANES
    last = pl.num_programs(1) - 1

    def run(rows):
        @pl.when(c == 0)
        def _():
            for k in range(groups):
                h_scr[0:rows, k * LANES:(k + 1) * LANES] = xs_ref[pl.ds(k, rows, stride=SUBLANES), :].astype(BF16)
            acc_scr[0:rows, :] = jnp.zeros((rows, acc_scr.shape[1]), F32)

        h = h_scr[0:rows, :]
        gate = _dot(h, wg_ref[...].astype(BF16))
        up = _dot(h, wu_ref[...].astype(BF16))
        act = (_silu(gate) * up).astype(BF16)
        acc_scr[0:rows, :] += _dot(act, wd_ref[...].astype(BF16))

        @pl.when(c == last)
        def _():
            for k in range(groups):
                ys_ref[pl.ds(k, rows, stride=SUBLANES), :] = acc_scr[0:rows, k * LANES:(k + 1) * LANES]
            if rows < tmx:
                ys_ref[rows * SUBLANES:, :] = jnp.zeros(((tmx - rows) * SUBLANES, LANES), F32)

    step = tmx // EXPERT_ROW_STEPS
    for q in range(1, EXPERT_ROW_STEPS + 1):
        pl.when(jnp.logical_and(rows_valid > (q - 1) * step, rows_valid <= q * step))(functools.partial(run, q * step))

    @pl.when(jnp.logical_and(rows_valid == 0, c == 0))
    def _():
        ys_ref[...] = jnp.zeros_like(ys_ref)


def _experts(L, xs, tile_expert, tile_rows, we_gu, we_down, tmx):
    d = L.D
    d_ff = we_down.shape[1]
    tf = 512
    n_ch = d_ff // tf
    n_tiles = xs.shape[0] // (tmx * SUBLANES)

    def chunk(c, tv, t):
        return jnp.where(tv[t] > 0, c, n_ch - 1)

    grid_spec = pltpu.PrefetchScalarGridSpec(
        num_scalar_prefetch=2,
        grid=(n_tiles, n_ch),
        in_specs=[pl.BlockSpec((tmx * SUBLANES, LANES), lambda t, c, te, tv: (t, 0)),
                  pl.BlockSpec((None, d, tf), lambda t, c, te, tv: (te[t], 0, chunk(c, tv, t))),
                  pl.BlockSpec((None, d, tf), lambda t, c, te, tv: (te[t], 0, n_ch + chunk(c, tv, t))),
                  pl.BlockSpec((None, tf, d), lambda t, c, te, tv: (te[t], chunk(c, tv, t), 0))],
        out_specs=pl.BlockSpec((tmx * SUBLANES, LANES), lambda t, c, te, tv: (t, 0)),
        scratch_shapes=[pltpu.VMEM((tmx, d), BF16), pltpu.VMEM((tmx, d), F32)],
    )
    return pl.pallas_call(
        functools.partial(_experts_kernel, tmx=tmx),
        out_shape=jax.ShapeDtypeStruct(xs.shape, F32),
        grid_spec=grid_spec,
        compiler_params=_params(("arbitrary", "arbitrary")),
        name="moe_experts",
    )(tile_expert, tile_rows, xs, we_gu, we_gu, we_down)


def _combine_kernel(pos_ref, ys_ref, route_ref, x_ref, mod_ref, out_ref, buf1, buf2, sem, *, tmc):
    def copies(t):
        dst1 = _row_block(buf1, t)
        dst2 = _row_block(buf2, t)
        return (pltpu.make_async_copy(_row_block(ys_ref, pos_ref[0, 2 * t]), dst1, sem),
                pltpu.make_async_copy(_row_block(ys_ref, pos_ref[0, 2 * t + 1]), dst2, sem))

    def start(t, carry):
        for k, cp in enumerate(copies(t)):
            cp.start(priority=k)
        return carry

    def wait(t, carry):
        for cp in copies(t):
            cp.wait()
        return carry

    lax.fori_loop(0, tmc, start, 0)
    lax.fori_loop(0, tmc, wait, 0)
    w1 = route_ref[:, R_W1:R_W1 + 1]
    w2 = route_ref[:, R_W2:R_W2 + 1]
    for k in range(out_ref.shape[1] // LANES):
        sl = slice(k * LANES, (k + 1) * LANES)
        y = w1 * buf1[pl.ds(k, tmc, stride=SUBLANES), :] + w2 * buf2[pl.ds(k, tmc, stride=SUBLANES), :]
        out_ref[:, sl] = x_ref[:, sl] + mod_ref[G2:G2 + 1, sl] * y


def _combine(L, ys, pos, route, x, mod, with_ctx):
    tmc, d = L.TC, L.D
    ratio = L.tm // tmc
    n_t = L.tiles(with_ctx) * ratio
    return pl.pallas_call(
        functools.partial(_combine_kernel, tmc=tmc),
        out_shape=jax.ShapeDtypeStruct((L.rows(with_ctx), d), F32),
        grid=(n_t,),
        in_specs=[pl.BlockSpec((None, 1, 2 * tmc), lambda i: (i, 0, 0), memory_space=pltpu.SMEM),
                  pl.BlockSpec(memory_space=pl.ANY),
                  pl.BlockSpec((tmc, LANES), lambda i: (i, 0)),
                  pl.BlockSpec((tmc, d), lambda i: (i, 0)),
                  pl.BlockSpec((None, SUBLANES, d), lambda i: (L.mod_group(i // ratio), 0, 0))],
        out_specs=pl.BlockSpec((tmc, d), lambda i: (i, 0)),
        scratch_shapes=[pltpu.VMEM((tmc * SUBLANES, LANES), F32), pltpu.VMEM((tmc * SUBLANES, LANES), F32),
                        pltpu.SemaphoreType.DMA(())],
        compiler_params=_params(("arbitrary",)),
        name="moe_combine",
    )(pos.reshape(-1, 1, 2 * tmc)[:n_t], ys, route, x, mod)


def _moe(L, x, mod, g, w_router, we_gu, we_down, with_ctx):
    tmx = L.tm
    n_tok = L.N if with_ctx else L.NL
    route, counts = _router(L, x, mod, g, w_router, with_ctx)
    cnt = counts[0, :N_EXPERTS].astype(jnp.int32)
    tiles_e = (cnt + tmx - 1) // tmx
    tile_end = jnp.cumsum(tiles_e)
    start = (tile_end - tiles_e) * tmx
    n_tiles = (2 * n_tok) // tmx + N_EXPERTS
    tile_id = jnp.arange(n_tiles, dtype=jnp.int32)
    tile_expert = jnp.minimum(jnp.sum(tile_id[:, None] >= tile_end[None, :], axis=1), N_EXPERTS - 1).astype(jnp.int32)
    first_tile = (tile_end - tiles_e)[tile_expert]
    tile_rows = jnp.clip(cnt[tile_expert] - (tile_id - first_tile) * tmx, 0, tmx)
    tile_rows = jnp.where(tile_id < tile_end[-1], tile_rows, 0).astype(jnp.int32)
    rt = route[:n_tok]
    sel = jnp.stack([rt[:, R_I1], rt[:, R_I2]], axis=1).astype(jnp.int32)
    rank = jnp.stack([rt[:, R_R1], rt[:, R_R2]], axis=1).astype(jnp.int32)
    pos = (jnp.sum(jnp.where(sel[..., None] == jnp.arange(N_EXPERTS), start, 0), axis=-1) + rank).reshape(-1)
    xs = _dispatch(L, x, mod, g, pos, n_tiles * tmx, with_ctx)
    ys = _experts(L, xs, tile_expert, tile_rows, we_gu, we_down, tmx)
    return _combine(L, ys, pos, route, x, mod, with_ctx)


def _rope_tables(L, lane_dim, lane_first, lane_freq, lane_is_col, half):
    t = jnp.arange(L.T, dtype=jnp.int32)
    rows = (t // GRID_W).astype(F32)
    cols = (t % GRID_W).astype(F32)
    inv_freq = jnp.exp(-math.log(ROPE_THETA) * jnp.arange(half, dtype=F32) / half)
    pos = jnp.where(lane_is_col[None, :], cols[:, None], rows[:, None])
    ang = pos * inv_freq[lane_freq][None, :]
    cos = jnp.where(lane_dim[None, :], jnp.cos(ang), 1.0)
    sin = jnp.where(lane_dim[None, :], jnp.sin(ang), 0.0)
    sin = jnp.where(lane_first[None, :], -sin, sin)
    ident = jnp.ones((L.tm, LANES), F32)
    return jnp.concatenate([cos, ident], axis=0), jnp.concatenate([sin, 0.0 * ident], axis=0)


def _gqa_lane_perm():
    quarter = HEAD_DIM // 4
    order = []
    for half in range(2):
        for head in range(2):
            for sec in range(2):
                base = head * HEAD_DIM + sec * (HEAD_DIM // 2) + half * quarter
                order += list(range(base, base + quarter))
    return jnp.array(order, dtype=jnp.int32)


def _permute_blocks(a, perm):
    lead, n = a.shape[:-1], a.shape[-1]
    return a.reshape(lead + (n // LANES, LANES))[..., perm].reshape(lead + (n,))


def _gqa_rope_tables(L):
    d = _gqa_lane_perm() % HEAD_DIM
    dd = d % (HEAD_DIM // 2)
    quarter = HEAD_DIM // 4
    return _rope_tables(L, d >= 0, dd < quarter, dd % quarter, d >= HEAD_DIM // 2, quarter)


def _mla_rope_tables(L):
    lane = jnp.arange(LANES)
    d = lane - NOPE_DIM
    in_rope = (d >= 0) & (d < ROPE_DIM)
    dd = d % (ROPE_DIM // 2)
    quarter = ROPE_DIM // 4
    return _rope_tables(L, in_rope, dd < quarter, dd % quarter, d >= ROPE_DIM // 2, quarter)


def _segment_matrix(bounds):
    lane = jnp.arange(LANES)
    m = jnp.zeros((LANES, LANES), F32)
    for lo, hi in bounds:
        inside = (lane >= lo) & (lane < hi)
        m = m + jnp.where(inside[:, None] & inside[None, :], 1.0 / (hi - lo), 0.0)
    return m.astype(BF16)


def _segment_matrix_by_id(ids):
    same = ids[:, None] == ids[None, :]
    return (same / jnp.sum(same, axis=1, keepdims=True)).astype(BF16)


def _natten_bias(rel_bias):
    n_h = rel_bias.shape[0]
    col = jnp.arange(GRID_W, dtype=jnp.int32)
    c0 = jnp.clip(col - WIN_C // 2, 0, GRID_W - WIN_C)
    col_ok = (col[None, :] >= c0[:, None]) & (col[None, :] < c0[:, None] + WIN_C)
    col_idx = jnp.clip(col[None, :] - col[:, None] + WIN_C - 1, 0, 2 * WIN_C - 2)
    full = jnp.where(col_ok[None, None], rel_bias[:, :, col_idx], NEG_INF) * LOG2E
    masked = jnp.full((n_h, GRID_W, GRID_W), NEG_INF * LOG2E, F32)
    half = WIN_R // 2
    kinds = []
    for kind in range(3):
        q_rows = []
        for j in range(NAT_ROWS):
            ru, wu = ((j, 0), (half + j, j), (WIN_R + j, half))[kind]
            blocks = [full[:, u - ru + WIN_R - 1] if wu <= u < wu + WIN_R else masked for u in range(NAT_SPAN)]
            q_rows.append(jnp.concatenate(blocks, axis=-1))
        kinds.append(jnp.concatenate(q_rows, axis=1))
    return jnp.stack(kinds, axis=1).astype(F32)


def _dup_heads(w, n_heads):
    k = w.shape[0]
    w = w.reshape(k, n_heads, HEAD_DIM)
    return jnp.concatenate([w, w], axis=-1).reshape(k, n_heads * LANES)


def _half_pad_heads(w, n_heads, second):
    k = w.shape[0]
    w = w.reshape(k, n_heads, HEAD_DIM)
    z = jnp.zeros_like(w)
    return jnp.concatenate([z, w] if second else [w, z], axis=-1).reshape(k, n_heads * LANES)


def _ones_row(n_heads, second):
    lane = jnp.arange(LANES)
    return jnp.tile((lane == ONES_LANE[1 if second else 0]).astype(F32), n_heads)


def _alternate_heads(first, second, n_heads):
    lead = first.shape[:-1]
    f = first.reshape(lead + (n_heads, LANES))
    g = second.reshape(lead + (n_heads, LANES))
    odd = (jnp.arange(n_heads) % 2 == 1)[:, None]
    return jnp.where(odd, g, f).reshape(lead + (n_heads * LANES,))


def _logit_bound(q_sq, k_sq, scale):
    return (jnp.sqrt(q_sq * k_sq) * (scale * LOG2E * BOUND_MARGIN)).reshape(1, 1).astype(F32)


def _gain_sq(gain):
    return gain.shape[0] * jnp.max(jnp.abs(gain)) ** 2


def _gqa_mixer(L, x, mod, g_mix, p, need_ctx, tables):
    w = p["w_qkv"]
    nq = A_HEADS * HEAD_DIM
    nkv = A_KV_HEADS * HEAD_DIM
    w_v = w[:, nq + nkv:]
    perm = _gqa_lane_perm()
    w_qk = jnp.concatenate([w[:, :nq], _dup_heads(w[:, nq:nq + nkv], A_KV_HEADS)], axis=1)
    w_p = jnp.concatenate([_permute_blocks(w_qk, perm), _half_pad_heads(w_v, A_KV_HEADS, False),
                           _half_pad_heads(w_v, A_KV_HEADS, True)], axis=1)
    n_blk = A_KV_HEADS * LANES
    scale = HEAD_DIM ** -0.5
    gain_qk = jnp.concatenate([jnp.tile(p["q_gain"], A_HEADS) * (scale * LOG2E), jnp.tile(p["k_gain"], 2 * A_KV_HEADS)])
    gain = jnp.concatenate([_permute_blocks(gain_qk, perm), jnp.ones((2 * n_blk,), F32)]).reshape(1, -1)
    colbias = jnp.concatenate([jnp.zeros((nq + n_blk,), F32), _ones_row(A_KV_HEADS, False),
                               _ones_row(A_KV_HEADS, True)]).reshape(1, -1)
    seg = _segment_matrix_by_id(perm // HEAD_DIM)
    tn = 512
    qkv = _proj(L, "gqa_qkv", x, w_p, tn=tn, mod=mod, g=g_mix, seg=seg, gain=gain, colbias=colbias,
                n_norm=(nq + n_blk) // tn, rope=(tables["gqa_cos"], tables["gqa_sin"], LANES // 2))
    q_blocks = nq // LANES
    k_blocks = n_blk // LANES
    rep_pairs = N_PAIRS // A_KV_HEADS
    bound = _logit_bound(_gain_sq(p["q_gain"]), _gain_sq(p["k_gain"]), scale)
    return _attention(L, "gqa_attn", bound, qkv, [lambda p_: p_], qkv, [lambda p_: q_blocks + p_ // rep_pairs],
                      qkv, [lambda p_: q_blocks + k_blocks + p_ // rep_pairs,
                            lambda p_: q_blocks + 2 * k_blocks + p_ // rep_pairs],
                      pair=True, q_head_lanes=HEAD_DIM // 2, mode="all" if need_ctx else "lat")


def _natten_mixer(L, x, mod, g_mix, p, need_ctx):
    n = B_HEADS * HEAD_DIM
    w = p["w_qkv"]
    w_v = w[:, 2 * n:]
    w_p = jnp.concatenate([w[:, :2 * n], _alternate_heads(_half_pad_heads(w_v, B_HEADS, False),
                                                          _half_pad_heads(w_v, B_HEADS, True), B_HEADS)], axis=1)
    scale = HEAD_DIM ** -0.5
    n_v = B_HEADS * LANES
    gain = jnp.concatenate([jnp.tile(p["q_gain"], B_HEADS) * (scale * LOG2E), jnp.tile(p["k_gain"], B_HEADS),
                            jnp.ones((n_v,), F32)]).reshape(1, -1)
    colbias = jnp.concatenate([jnp.zeros((2 * n,), F32),
                               _alternate_heads(_ones_row(B_HEADS, False), _ones_row(B_HEADS, True), B_HEADS)]
                              ).reshape(1, -1)
    seg = _segment_matrix([(0, HEAD_DIM), (HEAD_DIM, LANES)])
    tn = 512
    qkv = _proj(L, "nat_qkv", x, w_p, tn=tn, mod=mod, g=g_mix, seg=seg, gain=gain, colbias=colbias,
                n_norm=2 * n // tn)
    qk_bound = _logit_bound(_gain_sq(p["q_gain"]), _gain_sq(p["k_gain"]), scale)
    bound = qk_bound + jnp.maximum(jnp.max(p["rel_bias"]), 0.0) * LOG2E
    o = _natten(L, bound, qkv, _natten_bias(p["rel_bias"]))
    if need_ctx:
        o_ctx = _attention(L, "nat_ctx_attn", qk_bound, qkv, [lambda p_: p_], qkv, [lambda p_: N_PAIRS + p_],
                           qkv, [lambda p_: 2 * N_PAIRS + 2 * p_, lambda p_: 2 * N_PAIRS + 2 * p_ + 1],
                           pair=True, mode="ctx")
        o = (o, o_ctx)
    return o


def _mla_mixer(L, x, mod, g_mix, p, need_ctx, tables):
    d = L.D
    w_in = p["w_in"]
    n_c = Q_LORA + KV_LORA
    w_in_p = jnp.concatenate([w_in[:, :n_c], jnp.zeros((d, NOPE_DIM), F32), w_in[:, n_c:],
                              jnp.zeros((d, LANES - NOPE_DIM - ROPE_DIM), F32)], axis=1)
    dq = NOPE_DIM + ROPE_DIM
    pad_q = jnp.zeros((LANES - dq,), F32)
    seg_rope = _segment_matrix([(NOPE_DIM, dq)])
    gain_kr = jnp.concatenate([jnp.zeros((NOPE_DIM,), F32), p["k_gain"][NOPE_DIM:], pad_q]).reshape(1, LANES)
    cos, sin = tables["mla_cos"], tables["mla_sin"]
    cq, ckv, kr = _mla_in(L, x, mod, g_mix, w_in_p, p["g_dq"], p["g_dkv"], seg_rope, gain_kr, cos, sin)

    scale = dq ** -0.5
    w_uq = jnp.pad(p["w_uq"].reshape(Q_LORA, C_HEADS, dq), ((0, 0), (0, 0), (0, LANES - dq)))
    gain_q = jnp.tile(jnp.concatenate([p["q_gain"], pad_q]) * (scale * LOG2E), C_HEADS).reshape(1, -1)
    seg_q = _segment_matrix([(0, NOPE_DIM), (NOPE_DIM, dq)])
    n_hl = C_HEADS * LANES
    tn = 512
    q = _proj(L, "mla_q", cq, w_uq.reshape(Q_LORA, n_hl), tn=tn, seg=seg_q, gain=gain_q, n_norm=n_hl // tn,
              rope=(cos, sin, ROPE_DIM // 4))

    w_ukv = p["w_ukv"].reshape(KV_LORA, C_HEADS, NOPE_DIM + V_DIM)
    w_uk = jnp.pad(w_ukv[:, :, :NOPE_DIM], ((0, 0), (0, 0), (0, LANES - NOPE_DIM))).reshape(KV_LORA, n_hl)
    w_v = w_ukv[:, :, NOPE_DIM:].reshape(KV_LORA, C_HEADS * V_DIM)
    w_uv = _alternate_heads(_half_pad_heads(w_v, C_HEADS, False), _half_pad_heads(w_v, C_HEADS, True), C_HEADS)
    gain_k = jnp.tile(jnp.concatenate([p["k_gain"][:NOPE_DIM], jnp.zeros((LANES - NOPE_DIM,), F32)]),
                      C_HEADS).reshape(1, -1)
    seg_k = _segment_matrix([(0, NOPE_DIM)])
    ones = _alternate_heads(_ones_row(C_HEADS, False), _ones_row(C_HEADS, True), C_HEADS)
    kv = _proj(L, "mla_kv", ckv, jnp.concatenate([w_uk, w_uv], axis=1), tn=tn, seg=seg_k,
               gain=jnp.concatenate([gain_k, jnp.ones((1, n_hl), F32)], axis=1), n_norm=n_hl // tn, add=kr,
               colbias=jnp.concatenate([jnp.zeros((n_hl,), F32), ones]).reshape(1, -1))
    q_sq = _gain_sq(p["q_gain"][:NOPE_DIM]) + _gain_sq(p["q_gain"][NOPE_DIM:])
    k_sq = _gain_sq(p["k_gain"][:NOPE_DIM]) + _gain_sq(p["k_gain"][NOPE_DIM:])
    bound = _logit_bound(q_sq, k_sq, scale)
    heads = [lambda p_: 2 * p_, lambda p_: 2 * p_ + 1]
    v_heads = [lambda p_: C_HEADS + 2 * p_, lambda p_: C_HEADS + 2 * p_ + 1]
    return _attention(L, "mla_attn", bound, q, heads, kv, heads, kv, v_heads, pair=False,
                      mode="all" if need_ctx else "lat")


def kernel(x, c, ctx, c_ctx, l0_w_mod, l0_b_mod, l0_g_mix, l0_g_ffn, l0_w_qkv, l0_q_gain, l0_k_gain, l0_w_o, l0_w_gu, l0_w_down, l1_w_mod, l1_b_mod, l1_g_mix, l1_g_ffn, l1_w_qkv, l1_q_gain, l1_k_gain, l1_rel_bias, l1_w_o, l1_w_router, l1_we_gu, l1_we_down, l2_w_mod, l2_b_mod, l2_g_mix, l2_g_ffn, l2_w_in, l2_g_dq, l2_g_dkv, l2_w_uq, l2_w_ukv, l2_q_gain, l2_k_gain, l2_w_o, l2_w_gu, l2_w_down, l3_w_mod, l3_b_mod, l3_g_mix, l3_g_ffn, l3_w_qkv, l3_q_gain, l3_k_gain, l3_w_o, l3_w_router, l3_we_gu, l3_we_down):
    layers = [
        dict(w_mod=l0_w_mod, b_mod=l0_b_mod, g_mix=l0_g_mix, g_ffn=l0_g_ffn, kind="gqa",
             mix=dict(w_qkv=l0_w_qkv, q_gain=l0_q_gain, k_gain=l0_k_gain), w_o=l0_w_o,
             ffn=dict(w_gu=l0_w_gu, w_down=l0_w_down)),
        dict(w_mod=l1_w_mod, b_mod=l1_b_mod, g_mix=l1_g_mix, g_ffn=l1_g_ffn, kind="natten",
             mix=dict(w_qkv=l1_w_qkv, q_gain=l1_q_gain, k_gain=l1_k_gain, rel_bias=l1_rel_bias), w_o=l1_w_o,
             moe=dict(w_router=l1_w_router, we_gu=l1_we_gu, we_down=l1_we_down)),
        dict(w_mod=l2_w_mod, b_mod=l2_b_mod, g_mix=l2_g_mix, g_ffn=l2_g_ffn, kind="mla",
             mix=dict(w_in=l2_w_in, g_dq=l2_g_dq, g_dkv=l2_g_dkv, w_uq=l2_w_uq, w_ukv=l2_w_ukv,
                      q_gain=l2_q_gain, k_gain=l2_k_gain), w_o=l2_w_o,
             ffn=dict(w_gu=l2_w_gu, w_down=l2_w_down)),
        dict(w_mod=l3_w_mod, b_mod=l3_b_mod, g_mix=l3_g_mix, g_ffn=l3_g_ffn, kind="gqa",
             mix=dict(w_qkv=l3_w_qkv, q_gain=l3_q_gain, k_gain=l3_k_gain), w_o=l3_w_o,
             moe=dict(w_router=l3_w_router, we_gu=l3_we_gu, we_down=l3_we_down)),
    ]
    batch, seq, d_model = x.shape
    L = _Layout(batch, seq, ctx.shape[1], d_model)
    gqa_cos, gqa_sin = _gqa_rope_tables(L)
    mla_cos, mla_sin = _mla_rope_tables(L)
    tables = dict(gqa_cos=gqa_cos, gqa_sin=gqa_sin, mla_cos=mla_cos, mla_sin=mla_sin)
    cvec = jnp.concatenate([c, c_ctx[None, :], jnp.zeros((SUBLANES - batch - 1, d_model), F32)], axis=0)
    xs = (x.reshape(L.NL, d_model), ctx.reshape(L.NC, d_model))
    for li, p in enumerate(layers):
        need_ctx = li < len(layers) - 1
        mod = _adaln(cvec, p["w_mod"], p["b_mod"], batch + 1)
        if p["kind"] == "gqa":
            o = _gqa_mixer(L, xs, mod, p["g_mix"], p["mix"], need_ctx, tables)
        elif p["kind"] == "natten":
            o = _natten_mixer(L, xs, mod, p["g_mix"], p["mix"], need_ctx)
        else:
            o = _mla_mixer(L, xs, mod, p["g_mix"], p["mix"], need_ctx, tables)
        xs = _oproj(L, o, p["w_o"], xs, mod, need_ctx)
        if "ffn" in p:
            xs = _ffn(L, xs, mod, p["g_ffn"], p["ffn"]["w_gu"], p["ffn"]["w_down"], need_ctx)
        else:
            xs = _moe(L, xs, mod, p["g_ffn"], p["moe"]["w_router"], p["moe"]["we_gu"], p["moe"]["we_down"], need_ctx)
    return xs[:L.NL].reshape(batch, seq, d_model)
```

```python
import functools
import math

import jax
import jax.numpy as jnp
from jax import lax
from jax.experimental import pallas as pl
from jax.experimental.pallas import tpu as pltpu

F32 = jnp.float32
BF16 = jnp.bfloat16

LANES = 128
SUBLANES = 8
VMEM_LIMIT_BYTES = 56 * 1024 * 1024

NORM_EPS = 1e-6
ROPE_THETA = 10000.0
NEG_INF = -1e30
GRID_W = 64
HEAD_DIM = 64
WIN_R = 8
WIN_C = 16
A_HEADS = 16
A_KV_HEADS = 4
B_HEADS = 16
C_HEADS = 16
Q_LORA = 384
KV_LORA = 256
NOPE_DIM = 64
ROPE_DIM = 32
V_DIM = 64
N_EXPERTS = 8
N_PAIRS = 8
LOG2E = math.log2(math.e)
FAST_BOUND = 40.0
BOUND_MARGIN = 1.02
ONES_LANE = (HEAD_DIM, 0)
EXPERT_ROW_STEPS = 4
NAT_ROWS = 4
NAT_SPAN = NAT_ROWS + WIN_R

SH1, SC1, G1, SH2, SC2, G2 = range(6)


def _params(sem):
    return pltpu.CompilerParams(dimension_semantics=sem, vmem_limit_bytes=VMEM_LIMIT_BYTES)


def _dot(a, b):
    return jnp.dot(a, b, preferred_element_type=F32)


def _dot_nt(a, b):
    return lax.dot_general(a, b, (((1,), (1,)), ((), ())), preferred_element_type=F32)


def _dot3(a, b):
    a_hi, b_hi = a.astype(BF16), b.astype(BF16)
    a_lo, b_lo = (a - a_hi.astype(F32)).astype(BF16), (b - b_hi.astype(F32)).astype(BF16)
    return _dot(a_hi, b_hi) + (_dot(a_lo, b_hi) + _dot(a_hi, b_lo))


def _silu(x):
    return x * (1.0 / (1.0 + jnp.exp(-x)))


def _norm_mod(x, g, shift, scale):
    ms = jnp.mean(x * x, axis=-1, keepdims=True)
    y = x * lax.rsqrt(ms + NORM_EPS) * g
    return y * (1.0 + scale) + shift


def _seg_mean_sq(y, seg):
    return _dot((y * y).astype(BF16), seg)


def _rope(y, cos, sin_signed, shift):
    if 2 * shift == LANES:
        return y * cos + pltpu.roll(y, shift, 1) * sin_signed
    lane = lax.broadcasted_iota(jnp.int32, (1, y.shape[1]), 1)
    first = (lane & (2 * shift - 1)) < shift
    partner = jnp.where(first, pltpu.roll(y, LANES - shift, 1), pltpu.roll(y, shift, 1))
    return y * cos + partner * sin_signed


class _Layout:
    def __init__(self, batch, seq, ctx_len, d_model):
        self.B, self.T, self.TC, self.D = batch, seq, ctx_len, d_model
        self.NL = batch * seq
        self.NC = batch * ctx_len
        self.N = self.NL + self.NC
        self.tm = 1024 if (self.NC % 1024 == 0 and seq % 1024 == 0) else ctx_len
        assert seq % self.tm == 0 and self.NC % self.tm == 0
        self.tpb = seq // self.tm
        self.tq = ctx_len
        assert seq % self.tq == 0 and seq % GRID_W == 0

    def mod_group(self, i):
        return jnp.minimum(i // self.tpb, self.B)

    def rope_block(self, i):
        return jnp.where(i < self.B * self.tpb, i % self.tpb, self.tpb)

    def rows(self, with_ctx):
        return self.N if with_ctx else self.NL

    def tiles(self, with_ctx):
        return self.rows(with_ctx) // self.tm


def _adaln_kernel(c_ref, w_ref, b_ref, o_ref):
    s = _silu(c_ref[...])
    o_ref[...] = _dot3(s, w_ref[...]) + b_ref[...]


def _adaln(cvec, w_mod, b_mod, n_groups):
    d, n_out = w_mod.shape
    tn = n_out // 4
    out = pl.pallas_call(
        _adaln_kernel,
        out_shape=jax.ShapeDtypeStruct((SUBLANES, n_out), F32),
        grid=(n_out // tn,),
        in_specs=[pl.BlockSpec((SUBLANES, d), lambda j: (0, 0)),
                  pl.BlockSpec((d, tn), lambda j: (0, j)),
                  pl.BlockSpec((1, tn), lambda j: (0, j))],
        out_specs=pl.BlockSpec((SUBLANES, tn), lambda j: (0, j)),
        compiler_params=_params(("arbitrary",)),
        name="adaln",
    )(cvec, w_mod, b_mod.reshape(1, n_out))
    mod = out[:n_groups].reshape(n_groups, 6, d)
    return jnp.pad(mod, ((0, 0), (0, 2), (0, 0)))


def _proj_kernel(*refs, n_x, lat_tiles, has_mod, n_norm, n_col_tiles, rope_shift, has_add, has_colbias, groups):
    refs = list(refs)
    x_refs = [refs.pop(0) for _ in range(n_x)]
    x_ref = x_refs[0]
    if has_mod:
        mod_ref, g_ref = refs.pop(0), refs.pop(0)
    w_ref = refs.pop(0)
    if n_norm:
        seg_ref, gain_ref = refs.pop(0), refs.pop(0)
    if rope_shift:
        cos_ref, sin_ref = refs.pop(0), refs.pop(0)
    if has_add:
        add_ref = refs.pop(0)
    if has_colbias:
        colbias_ref = refs.pop(0)
    o_ref = refs.pop(0)
    j = pl.program_id(1)

    if has_mod:
        h_scr = refs.pop(0)

        def fill(part):
            h = _norm_mod(x_refs[part][...], g_ref[...], mod_ref[SH1:SH1 + 1, :], mod_ref[SC1:SC1 + 1, :])
            h_scr[...] = h.astype(BF16)

        pl.when(j == 0)(functools.partial(_per_part, n_x, lat_tiles, fill))
        lhs_ref = h_scr
    else:
        lhs_ref = x_ref

    def matmul():
        return _dot(lhs_ref[...], w_ref[...].astype(BF16))

    def normed():
        acc = matmul()
        for c in range(groups):
            sl = slice(c * LANES, (c + 1) * LANES)
            y = acc[:, sl]
            y = y * lax.rsqrt(_seg_mean_sq(y, seg_ref[...]) + NORM_EPS) * gain_ref[:, sl]
            if rope_shift:
                y = _rope(y, cos_ref[...], sin_ref[...], rope_shift)
            if has_add:
                y = y + add_ref[...]
            o_ref[:, sl] = y.astype(o_ref.dtype)

    def plain():
        acc = matmul()
        y = acc + colbias_ref[...] if has_colbias else acc
        o_ref[...] = y.astype(o_ref.dtype)

    if n_norm == 0:
        plain()
    elif n_norm == n_col_tiles:
        normed()
    else:
        pl.when(j < n_norm)(normed)
        pl.when(j >= n_norm)(plain)


def _proj(L, name, x, w, *, with_ctx=True, tn=512, mod=None, g=None, seg=None, gain=None, n_norm=0,
          rope=None, add=None, colbias=None, out_dtype=BF16):
    k, n_out = w.shape
    tm = L.tm
    n_col = n_out // tn
    x = _as_parts(x)
    assert len(x) == 1 or mod is not None
    lat_tiles = L.NL // tm
    in_specs = _row_specs(x, tm, lat_tiles)
    args = list(x)
    scratch = []
    if mod is not None:
        in_specs += [pl.BlockSpec((None, SUBLANES, k), lambda i, j: (L.mod_group(i), 0, 0)),
                     pl.BlockSpec((1, k), lambda i, j: (0, 0))]
        args += [mod, g.reshape(1, k)]
        scratch.append(pltpu.VMEM((tm, k), BF16))
    in_specs.append(pl.BlockSpec((k, tn), lambda i, j: (0, j)))
    args.append(w)
    if n_norm:
        in_specs += [pl.BlockSpec((LANES, LANES), lambda i, j: (0, 0)),
                     pl.BlockSpec((1, tn), lambda i, j: (0, j))]
        args += [seg, gain]
    rope_shift = 0
    if rope is not None:
        cos, sin, rope_shift = rope
        in_specs += [pl.BlockSpec((tm, LANES), lambda i, j: (L.rope_block(i), 0))] * 2
        args += [cos, sin]
    if add is not None:
        in_specs.append(pl.BlockSpec((tm, LANES), lambda i, j: (i, 0)))
        args.append(add)
    if colbias is not None:
        in_specs.append(pl.BlockSpec((1, tn), lambda i, j: (0, j)))
        args.append(colbias)
    kern = functools.partial(_proj_kernel, n_x=len(x), lat_tiles=lat_tiles, has_mod=mod is not None, n_norm=n_norm, n_col_tiles=n_col,
                             rope_shift=rope_shift, has_add=add is not None, has_colbias=colbias is not None,
                             groups=tn // LANES)
    return pl.pallas_call(
        kern,
        out_shape=jax.ShapeDtypeStruct((L.N, n_out), out_dtype),
        grid=(L.tiles(with_ctx), n_col),
        in_specs=in_specs,
        out_specs=pl.BlockSpec((tm, tn), lambda i, j: (i, j)),
        scratch_shapes=scratch,
        compiler_params=_params(("parallel", "arbitrary")),
        name=name,
    )(*args)


def _mla_in_kernel(x_ref, mod_ref, g_ref, w_ref, gdq_ref, gdkv_ref, seg_ref, gain_ref, cos_ref, sin_ref,
                   cq_ref, ckv_ref, kr_ref):
    h = _norm_mod(x_ref[...], g_ref[...], mod_ref[SH1:SH1 + 1, :], mod_ref[SC1:SC1 + 1, :])
    acc = _dot(h.astype(BF16), w_ref[...].astype(BF16))
    cq = acc[:, :Q_LORA]
    cq = cq * lax.rsqrt(jnp.mean(cq * cq, axis=-1, keepdims=True) + NORM_EPS) * gdq_ref[...]
    cq_ref[...] = cq.astype(BF16)
    ckv = acc[:, Q_LORA:Q_LORA + KV_LORA]
    ckv = ckv * lax.rsqrt(jnp.mean(ckv * ckv, axis=-1, keepdims=True) + NORM_EPS) * gdkv_ref[...]
    ckv_ref[...] = ckv.astype(BF16)
    kr = acc[:, Q_LORA + KV_LORA:]
    kr = kr * lax.rsqrt(_seg_mean_sq(kr, seg_ref[...]) + NORM_EPS) * gain_ref[...]
    kr_ref[...] = _rope(kr, cos_ref[...], sin_ref[...], ROPE_DIM // 4)


def _mla_in(L, x, mod, g, w_in_p, g_dq, g_dkv, seg_kr, gain_kr, cos, sin):
    tm, d = L.tm, L.D
    n_out = w_in_p.shape[1]
    full = lambda shape: pl.BlockSpec(shape, lambda i: (0,) * len(shape))
    return pl.pallas_call(
        _mla_in_kernel,
        out_shape=(jax.ShapeDtypeStruct((L.N, Q_LORA), BF16),
                   jax.ShapeDtypeStruct((L.N, KV_LORA), BF16),
                   jax.ShapeDtypeStruct((L.N, LANES), F32)),
        grid=(L.tiles(True),),
        in_specs=[pl.BlockSpec((tm, d), lambda i: (i, 0)),
                  pl.BlockSpec((None, SUBLANES, d), lambda i: (L.mod_group(i), 0, 0)),
                  full((1, d)), full((d, n_out)), full((1, Q_LORA)), full((1, KV_LORA)),
                  full((LANES, LANES)), full((1, LANES)),
                  pl.BlockSpec((tm, LANES), lambda i: (L.rope_block(i), 0)),
                  pl.BlockSpec((tm, LANES), lambda i: (L.rope_block(i), 0))],
        out_specs=(pl.BlockSpec((tm, Q_LORA), lambda i: (i, 0)),
                   pl.BlockSpec((tm, KV_LORA), lambda i: (i, 0)),
                   pl.BlockSpec((tm, LANES), lambda i: (i, 0))),
        compiler_params=_params(("parallel",)),
        name="mla_in",
    )(x, mod, g.reshape(1, d), w_in_p, g_dq.reshape(1, -1), g_dkv.reshape(1, -1), seg_kr, gain_kr, cos, sin)


def _softmax_pv(s_list, v_list, bound, ones_lane, fast):
    if fast:
        o = None
        for s, v in zip(s_list, v_list):
            part = _dot(jnp.exp2(s - bound).astype(BF16), v)
            o = part if o is None else o + part
        return o / o[:, ones_lane:ones_lane + 1]
    m = None
    for s in s_list:
        ms = jnp.max(s, axis=-1, keepdims=True)
        m = ms if m is None else jnp.maximum(m, ms)
    o, den = None, None
    for s, v in zip(s_list, v_list):
        p = jnp.exp2(s - m)
        ds = jnp.sum(p, axis=-1, keepdims=True)
        part = _dot(p.astype(BF16), v)
        o = part if o is None else o + part
        den = ds if den is None else den + ds
    return o / den


def _attn_kernel(*refs, nq, nk, pair, q_head_lanes, lat_keys, nq_lat, ctx_q):
    refs = list(refs)
    bound_ref = refs.pop(0)
    q_refs = [refs.pop(0) for _ in range(nq)]
    kl_refs = [refs.pop(0) for _ in range(nk)] if lat_keys else []
    kc_refs = [refs.pop(0) for _ in range(nk)]
    vl_refs = [refs.pop(0) for _ in range(2)] if lat_keys else []
    vc_refs = [refs.pop(0) for _ in range(2)]
    o_ref = refs.pop(0)
    lane = lax.broadcasted_iota(jnp.int32, (1, LANES), 1)
    lo = lane < HEAD_DIM
    q_first = (lane & (2 * q_head_lanes - 1)) < q_head_lanes
    bound = bound_ref[0, 0]

    def run(use_lat, fast):
        outs = []
        for hh in range(2):
            q = q_refs[hh % nq][...]
            if pair:
                q = jnp.where(q_first if hh == 0 else jnp.logical_not(q_first), q, jnp.zeros_like(q))
            s_list = [_dot_nt(q, kc_refs[hh % nk][...])]
            v_list = [vc_refs[hh][...]]
            if use_lat:
                s_list.append(_dot_nt(q, kl_refs[hh % nk][...]))
                v_list.append(vl_refs[hh][...])
            outs.append(_softmax_pv(s_list, v_list, bound, ONES_LANE[hh], fast))
        o_ref[...] = jnp.where(lo, outs[0], outs[1]).astype(o_ref.dtype)

    is_fast = bound <= FAST_BOUND
    for fast in (True, False):
        pred = is_fast if fast else jnp.logical_not(is_fast)
        if lat_keys and ctx_q:
            i = pl.program_id(2)
            pl.when(jnp.logical_and(pred, i < nq_lat))(functools.partial(run, True, fast))
            pl.when(jnp.logical_and(pred, i == nq_lat))(functools.partial(run, False, fast))
        else:
            pl.when(pred)(functools.partial(run, lat_keys, fast))


def _attention(L, name, bound, q, q_cols, k, k_cols, v, v_cols, *, pair, mode, q_head_lanes=HEAD_DIM):
    B, T, TC = L.B, L.T, L.TC
    tq = 2 * L.tq if (mode == "lat" and T % (2 * L.tq) == 0) else L.tq
    nq_lat = T // tq
    ctx_row = L.NL // TC
    lat_keys = mode != "ctx"
    ctx_q = mode != "lat"
    n_i = {"all": nq_lat + 1, "lat": nq_lat, "ctx": 1}[mode]

    def q_row(b, i):
        if mode == "ctx":
            return ctx_row + b
        if mode == "lat":
            return b * nq_lat + i
        return jnp.where(i < nq_lat, b * nq_lat + i, ctx_row + b)

    in_specs = [pl.BlockSpec(memory_space=pltpu.SMEM)]
    args = [bound]
    for f in q_cols:
        in_specs.append(pl.BlockSpec((tq, LANES), lambda b, p, i, f=f: (q_row(b, i), f(p))))
        args.append(q)
    if lat_keys:
        for f in k_cols:
            in_specs.append(pl.BlockSpec((T, LANES), lambda b, p, i, f=f: (b, f(p))))
            args.append(k)
    for f in k_cols:
        in_specs.append(pl.BlockSpec((TC, LANES), lambda b, p, i, f=f: (ctx_row + b, f(p))))
        args.append(k)
    if lat_keys:
        for f in v_cols:
            in_specs.append(pl.BlockSpec((T, LANES), lambda b, p, i, f=f: (b, f(p))))
            args.append(v)
    for f in v_cols:
        in_specs.append(pl.BlockSpec((TC, LANES), lambda b, p, i, f=f: (ctx_row + b, f(p))))
        args.append(v)
    out_rows = {"all": L.N, "lat": L.NL, "ctx": L.NC}[mode]
    out_row = (lambda b, i: b) if mode == "ctx" else q_row
    kern = functools.partial(_attn_kernel, nq=len(q_cols), nk=len(k_cols), pair=pair, q_head_lanes=q_head_lanes,
                             lat_keys=lat_keys, nq_lat=nq_lat, ctx_q=ctx_q)
    return pl.pallas_call(
        kern,
        out_shape=jax.ShapeDtypeStruct((out_rows, N_PAIRS * LANES), BF16),
        grid=(B, N_PAIRS, n_i),
        in_specs=in_specs,
        out_specs=pl.BlockSpec((tq, LANES), lambda b, p, i: (out_row(b, i), p)),
        compiler_params=_params(("parallel", "parallel", "arbitrary")),
        name=name,
    )(*args)


def _natten_kernel(bound_ref, q_ref, k_ref, va_ref, vb_ref, kc_ref, vca_ref, vcb_ref, bias_ref, o_ref, oc_scr, *,
                   rows_n):
    lo = lax.broadcasted_iota(jnp.int32, (1, LANES), 1) < HEAD_DIM
    halves = (lo, jnp.logical_not(lo))
    v_refs = (va_ref, vb_ref)
    vc_refs = (vca_ref, vcb_ref)
    n_groups = rows_n // NAT_ROWS
    bound = bound_ref[0, 0]

    def group(g):
        u0 = jnp.clip(g * NAT_ROWS - WIN_R // 2, 0, rows_n - NAT_SPAN)
        kind = jnp.where(g == 0, 0, jnp.where(g == n_groups - 1, 2, 1))
        rows = pl.ds(pl.multiple_of(g * (NAT_ROWS * GRID_W), NAT_ROWS * GRID_W), NAT_ROWS * GRID_W)
        keys = pl.ds(pl.multiple_of(u0 * GRID_W, GRID_W), NAT_SPAN * GRID_W)
        return kind, rows, keys

    def masked(q, hh):
        return jnp.where(halves[hh], q, jnp.zeros_like(q))

    def fast_body(g, carry):
        kind, rows, keys = group(g)
        q = q_ref[rows, :]
        kw = k_ref[keys, :]
        outs = []
        for hh in range(2):
            s_l = _dot_nt(masked(q, hh), kw) + bias_ref[hh, kind]
            o = oc_scr[hh, rows, :] + _dot(jnp.exp2(s_l - bound).astype(BF16), v_refs[hh][keys, :])
            outs.append(o / o[:, ONES_LANE[hh]:ONES_LANE[hh] + 1])
        o_ref[rows, :] = jnp.where(lo, outs[0], outs[1]).astype(o_ref.dtype)
        return carry

    def exact_body(g, carry):
        kind, rows, keys = group(g)
        q = q_ref[rows, :]
        kw = k_ref[keys, :]
        outs = []
        for hh in range(2):
            qh = masked(q, hh)
            s_l = _dot_nt(qh, kw) + bias_ref[hh, kind]
            s_c = _dot_nt(qh, kc_ref[...])
            outs.append(_softmax_pv([s_l, s_c], [v_refs[hh][keys, :], vc_refs[hh][...]], bound, ONES_LANE[hh], False))
        o_ref[rows, :] = jnp.where(lo, outs[0], outs[1]).astype(o_ref.dtype)
        return carry

    is_fast = bound <= FAST_BOUND

    @pl.when(is_fast)
    def _():
        q_all = q_ref[...]
        for hh in range(2):
            p_c = jnp.exp2(_dot_nt(masked(q_all, hh), kc_ref[...]) - bound).astype(BF16)
            oc_scr[hh] = _dot(p_c, vc_refs[hh][...])
        lax.fori_loop(0, n_groups, fast_body, 0, unroll=2)

    @pl.when(jnp.logical_not(is_fast))
    def _():
        lax.fori_loop(0, n_groups, exact_body, 0)


def _natten(L, bound, qkv, bias):
    B, T, TC = L.B, L.T, L.TC
    ctx_row = L.NL // TC
    rows_n = T // GRID_W
    assert rows_n % NAT_ROWS == 0 and rows_n >= NAT_SPAN and NAT_ROWS == WIN_R // 2
    k0, v0 = N_PAIRS, 2 * N_PAIRS
    return pl.pallas_call(
        functools.partial(_natten_kernel, rows_n=T // GRID_W),
        out_shape=jax.ShapeDtypeStruct((L.NL, N_PAIRS * LANES), BF16),
        grid=(B, N_PAIRS),
        in_specs=[pl.BlockSpec(memory_space=pltpu.SMEM),
                  pl.BlockSpec((T, LANES), lambda b, p: (b, p)),
                  pl.BlockSpec((T, LANES), lambda b, p: (b, k0 + p)),
                  pl.BlockSpec((T, LANES), lambda b, p: (b, v0 + 2 * p)),
                  pl.BlockSpec((T, LANES), lambda b, p: (b, v0 + 2 * p + 1)),
                  pl.BlockSpec((TC, LANES), lambda b, p: (ctx_row + b, k0 + p)),
                  pl.BlockSpec((TC, LANES), lambda b, p: (ctx_row + b, v0 + 2 * p)),
                  pl.BlockSpec((TC, LANES), lambda b, p: (ctx_row + b, v0 + 2 * p + 1)),
                  pl.BlockSpec((2, 3, NAT_ROWS * GRID_W, NAT_SPAN * GRID_W), lambda b, p: (p, 0, 0, 0))],
        out_specs=pl.BlockSpec((T, LANES), lambda b, p: (b, p)),
        scratch_shapes=[pltpu.VMEM((2, T, LANES), F32)],
        compiler_params=_params(("parallel", "parallel")),
        name="natten",
    )(bound, qkv, qkv, qkv, qkv, qkv, qkv, qkv, bias)


def _row_specs(arrs, tm, lat_tiles):
    if len(arrs) == 1:
        return [pl.BlockSpec((tm, arrs[0].shape[1]), lambda i, *_: (i, 0))]
    lat, ctx = arrs
    return [pl.BlockSpec((tm, lat.shape[1]), lambda i, *_: (jnp.minimum(i, lat_tiles - 1), 0)),
            pl.BlockSpec((tm, ctx.shape[1]), lambda i, *_: (jnp.maximum(i - lat_tiles, 0), 0))]


def _per_part(n_parts, lat_tiles, fn):
    if n_parts == 1:
        fn(0)
    else:
        is_lat = pl.program_id(0) < lat_tiles
        pl.when(is_lat)(functools.partial(fn, 0))
        pl.when(jnp.logical_not(is_lat))(functools.partial(fn, 1))


def _as_parts(a):
    return list(a) if isinstance(a, (tuple, list)) else [a]


def _oproj_kernel(*refs, n_o, n_x, lat_tiles):
    refs = list(refs)
    o_refs = [refs.pop(0) for _ in range(n_o)]
    w_ref = refs.pop(0)
    x_refs = [refs.pop(0) for _ in range(n_x)]
    mod_ref, out_ref = refs

    def run(part):
        acc = _dot(o_refs[min(part, n_o - 1)][...], w_ref[...].astype(BF16))
        out_ref[...] = x_refs[min(part, n_x - 1)][...] + mod_ref[G1:G1 + 1, :] * acc

    _per_part(max(n_o, n_x), lat_tiles, run)


def _oproj(L, o, w_o, x, mod, with_ctx):
    tm, d = L.tm // 2, L.D
    k = w_o.shape[0]
    ratio = L.tm // tm
    o, x = _as_parts(o), _as_parts(x)
    lat_tiles = L.NL // tm
    return pl.pallas_call(
        functools.partial(_oproj_kernel, n_o=len(o), n_x=len(x), lat_tiles=lat_tiles),
        out_shape=jax.ShapeDtypeStruct((L.rows(with_ctx), d), F32),
        grid=(L.tiles(with_ctx) * ratio,),
        in_specs=_row_specs(o, tm, lat_tiles) + [pl.BlockSpec((k, d), lambda i: (0, 0))]
        + _row_specs(x, tm, lat_tiles)
        + [pl.BlockSpec((None, SUBLANES, d), lambda i: (L.mod_group(i // ratio), 0, 0))],
        out_specs=pl.BlockSpec((tm, d), lambda i: (i, 0)),
        compiler_params=_params(("parallel",)),
        name="oproj",
    )(*o, w_o, *x, mod)


def _ffn_kernel(x_ref, mod_ref, g_ref, wgu_ref, wd_ref, out_ref, h_scr, acc_scr, *, d_ff, tf):
    h = _norm_mod(x_ref[...], g_ref[...], mod_ref[SH2:SH2 + 1, :], mod_ref[SC2:SC2 + 1, :])
    h_scr[...] = h.astype(BF16)
    for c in range(d_ff // tf):
        h = h_scr[...]
        gate = _dot(h, wgu_ref[:, c * tf:(c + 1) * tf])
        up = _dot(h, wgu_ref[:, d_ff + c * tf:d_ff + (c + 1) * tf])
        act = (_silu(gate) * up).astype(BF16)
        part = _dot(act, wd_ref[c * tf:(c + 1) * tf, :])
        if c == 0:
            acc_scr[...] = part
        else:
            acc_scr[...] += part
    out_ref[...] = x_ref[...] + mod_ref[G2:G2 + 1, :] * acc_scr[...]


def _ffn(L, x, mod, g, w_gu, w_down, with_ctx):
    tm, d = L.tm, L.D
    d_ff = w_down.shape[0]
    resident = lambda shape: pl.BlockSpec(shape, lambda i: (0, 0), pipeline_mode=pl.Buffered(1))
    return pl.pallas_call(
        functools.partial(_ffn_kernel, d_ff=d_ff, tf=256),
        out_shape=jax.ShapeDtypeStruct((L.rows(with_ctx), d), F32),
        grid=(L.tiles(with_ctx),),
        in_specs=[pl.BlockSpec((tm, d), lambda i: (i, 0)),
                  pl.BlockSpec((None, SUBLANES, d), lambda i: (L.mod_group(i), 0, 0)),
                  pl.BlockSpec((1, d), lambda i: (0, 0)),
                  resident((d, 2 * d_ff)),
                  resident((d_ff, d))],
        out_specs=pl.BlockSpec((tm, d), lambda i: (i, 0)),
        scratch_shapes=[pltpu.VMEM((tm, d), BF16), pltpu.VMEM((tm, d), F32)],
        compiler_params=_params(("parallel",)),
        name="ffn",
    )(x, mod, g.reshape(1, d), w_gu.astype(BF16), w_down.astype(BF16))


R_I1, R_I2, R_W1, R_W2, R_R1, R_R2 = range(6)


def _router_kernel(x_ref, mod_ref, g_ref, wr_ref, route_ref, cnt_ref, carry_scr, *, tm):
    i = pl.program_id(0)

    @pl.when(i == 0)
    def _():
        carry_scr[...] = jnp.zeros_like(carry_scr)

    h = _norm_mod(x_ref[...], g_ref[...], mod_ref[SH2:SH2 + 1, :], mod_ref[SC2:SC2 + 1, :])
    logits = _dot3(h, wr_ref[...])
    lane = lax.broadcasted_iota(jnp.int32, logits.shape, 1).astype(F32)
    lg = jnp.where(lane < N_EXPERTS, logits, -jnp.inf)
    m1 = jnp.max(lg, axis=-1, keepdims=True)
    i1 = jnp.min(jnp.where(lg == m1, lane, float(LANES)), axis=-1, keepdims=True)
    lg2 = jnp.where(lane == i1, -jnp.inf, lg)
    m2 = jnp.max(lg2, axis=-1, keepdims=True)
    i2 = jnp.min(jnp.where(lg2 == m2, lane, float(LANES)), axis=-1, keepdims=True)
    e = jnp.exp(m2 - m1)
    w1 = 1.0 / (1.0 + e)
    w2 = e / (1.0 + e)
    onehot = jnp.where((lane == i1) | (lane == i2), 1.0, 0.0)
    row = lax.broadcasted_iota(jnp.int32, (tm, tm), 0)
    col = lax.broadcasted_iota(jnp.int32, (tm, tm), 1)
    tri = jnp.where(row > col, 1.0, 0.0).astype(BF16)
    rank = _dot(tri, onehot.astype(BF16)) + carry_scr[0:1, :]
    r1 = jnp.sum(jnp.where(lane == i1, rank, 0.0), axis=-1, keepdims=True)
    r2 = jnp.sum(jnp.where(lane == i2, rank, 0.0), axis=-1, keepdims=True)
    carry_scr[...] = carry_scr[...] + jnp.sum(onehot, axis=0, keepdims=True)
    rec = jnp.zeros_like(logits)
    for pos, val in ((R_I1, i1), (R_I2, i2), (R_W1, w1), (R_W2, w2), (R_R1, r1), (R_R2, r2)):
        rec = jnp.where(lane == float(pos), val, rec)
    route_ref[...] = rec
    cnt_ref[...] = carry_scr[...]


def _router(L, x, mod, g, w_router, with_ctx):
    tm, d = L.tm, L.D
    wr = jnp.pad(w_router, ((0, 0), (0, LANES - w_router.shape[1])))
    return pl.pallas_call(
        functools.partial(_router_kernel, tm=tm),
        out_shape=(jax.ShapeDtypeStruct((L.rows(with_ctx), LANES), F32),
                   jax.ShapeDtypeStruct((SUBLANES, LANES), F32)),
        grid=(L.tiles(with_ctx),),
        in_specs=[pl.BlockSpec((tm, d), lambda i: (i, 0)),
                  pl.BlockSpec((None, SUBLANES, d), lambda i: (L.mod_group(i), 0, 0)),
                  pl.BlockSpec((1, d), lambda i: (0, 0)),
                  pl.BlockSpec((d, LANES), lambda i: (0, 0))],
        out_specs=(pl.BlockSpec((tm, LANES), lambda i: (i, 0)),
                   pl.BlockSpec((SUBLANES, LANES), lambda i: (0, 0))),
        scratch_shapes=[pltpu.VMEM((SUBLANES, LANES), F32)],
        compiler_params=_params(("arbitrary",)),
        name="router",
    )(x, mod, g.reshape(1, d), wr)


def _row_block(ref, row):
    return ref.at[pl.ds(pl.multiple_of(row * SUBLANES, SUBLANES), SUBLANES)]


def _dispatch_kernel(pos_ref, x_ref, mod_ref, g_ref, xs_in_ref, xs_ref, rows_scr, sem, *, tmd):
    del xs_in_ref
    h = _norm_mod(x_ref[...], g_ref[...], mod_ref[SH2:SH2 + 1, :], mod_ref[SC2:SC2 + 1, :])
    for c in range(h.shape[1] // LANES):
        rows_scr[pl.ds(c, tmd, stride=SUBLANES), :] = h[:, c * LANES:(c + 1) * LANES]

    def copies(t):
        src = _row_block(rows_scr, t)
        return (pltpu.make_async_copy(src, _row_block(xs_ref, pos_ref[0, 2 * t]), sem),
                pltpu.make_async_copy(src, _row_block(xs_ref, pos_ref[0, 2 * t + 1]), sem))

    def start(t, carry):
        for k, cp in enumerate(copies(t)):
            cp.start(priority=k)
        return carry

    def wait(t, carry):
        for cp in copies(t):
            cp.wait()
        return carry

    lax.fori_loop(0, tmd, start, 0)
    lax.fori_loop(0, tmd, wait, 0)


def _dispatch(L, x, mod, g, pos, n_slots, with_ctx):
    tmd, d = L.tm, L.D
    n_t = L.tiles(with_ctx)
    xs0 = jnp.zeros((n_slots * SUBLANES, LANES), F32)
    return pl.pallas_call(
        functools.partial(_dispatch_kernel, tmd=tmd),
        out_shape=jax.ShapeDtypeStruct(xs0.shape, F32),
        grid=(n_t,),
        in_specs=[pl.BlockSpec((None, 1, 2 * tmd), lambda i: (i, 0, 0), memory_space=pltpu.SMEM),
                  pl.BlockSpec((tmd, d), lambda i: (i, 0)),
                  pl.BlockSpec((None, SUBLANES, d), lambda i: (L.mod_group(i), 0, 0)),
                  pl.BlockSpec((1, d), lambda i: (0, 0)),
                  pl.BlockSpec(memory_space=pl.ANY)],
        out_specs=pl.BlockSpec(memory_space=pl.ANY),
        scratch_shapes=[pltpu.VMEM((tmd * SUBLANES, LANES), F32), pltpu.SemaphoreType.DMA(())],
        input_output_aliases={4: 0},
        compiler_params=pltpu.CompilerParams(dimension_semantics=("arbitrary",), has_side_effects=True,
                                             vmem_limit_bytes=VMEM_LIMIT_BYTES),
        name="moe_dispatch",
    )(pos.reshape(-1, 1, 2 * tmd)[:n_t], x, mod, g.reshape(1, d), xs0)


def _experts_kernel(te_ref, tr_ref, xs_ref, wg_ref, wu_ref, wd_ref, ys_ref, h_scr, acc_scr, *, tmx):
    del te_ref
    t, c = pl.program_id(0), pl.program_id(1)
    rows_valid = tr_ref[t]
    groups = h_scr.shape[1] // LANES
    last = pl.num_programs(1) - 1

    def run(rows):
        @pl.when(c == 0)
        def _():
            for k in range(groups):
                h_scr[0:rows, k * LANES:(k + 1) * LANES] = xs_ref[pl.ds(k, rows, stride=SUBLANES), :].astype(BF16)
            acc_scr[0:rows, :] = jnp.zeros((rows, acc_scr.shape[1]), F32)

        h = h_scr[0:rows, :]
        gate = _dot(h, wg_ref[...].astype(BF16))
        up = _dot(h, wu_ref[...].astype(BF16))
        act = (_silu(gate) * up).astype(BF16)
        acc_scr[0:rows, :] += _dot(act, wd_ref[...].astype(BF16))

        @pl.when(c == last)
        def _():
            for k in range(groups):
                ys_ref[pl.ds(k, rows, stride=SUBLANES), :] = acc_scr[0:rows, k * LANES:(k + 1) * LANES]
            if rows < tmx:
                ys_ref[rows * SUBLANES:, :] = jnp.zeros(((tmx - rows) * SUBLANES, LANES), F32)

    step = tmx // EXPERT_ROW_STEPS
    for q in range(1, EXPERT_ROW_STEPS + 1):
        pl.when(jnp.logical_and(rows_valid > (q - 1) * step, rows_valid <= q * step))(functools.partial(run, q * step))

    @pl.when(jnp.logical_and(rows_valid == 0, c == 0))
    def _():
        ys_ref[...] = jnp.zeros_like(ys_ref)


def _experts(L, xs, tile_expert, tile_rows, we_gu, we_down, tmx):
    d = L.D
    d_ff = we_down.shape[1]
    tf = 512
    n_ch = d_ff // tf
    n_tiles = xs.shape[0] // (tmx * SUBLANES)

    def chunk(c, tv, t):
        return jnp.where(tv[t] > 0, c, n_ch - 1)

    grid_spec = pltpu.PrefetchScalarGridSpec(
        num_scalar_prefetch=2,
        grid=(n_tiles, n_ch),
        in_specs=[pl.BlockSpec((tmx * SUBLANES, LANES), lambda t, c, te, tv: (t, 0)),
                  pl.BlockSpec((None, d, tf), lambda t, c, te, tv: (te[t], 0, chunk(c, tv, t))),
                  pl.BlockSpec((None, d, tf), lambda t, c, te, tv: (te[t], 0, n_ch + chunk(c, tv, t))),
                  pl.BlockSpec((None, tf, d), lambda t, c, te, tv: (te[t], chunk(c, tv, t), 0))],
        out_specs=pl.BlockSpec((tmx * SUBLANES, LANES), lambda t, c, te, tv: (t, 0)),
        scratch_shapes=[pltpu.VMEM((tmx, d), BF16), pltpu.VMEM((tmx, d), F32)],
    )
    return pl.pallas_call(
        functools.partial(_experts_kernel, tmx=tmx),
        out_shape=jax.ShapeDtypeStruct(xs.shape, F32),
        grid_spec=grid_spec,
        compiler_params=_params(("arbitrary", "arbitrary")),
        name="moe_experts",
    )(tile_expert, tile_rows, xs, we_gu, we_gu, we_down)


def _combine_kernel(pos_ref, ys_ref, route_ref, x_ref, mod_ref, out_ref, buf1, buf2, sem, *, tmc):
    def copies(t):
        dst1 = _row_block(buf1, t)
        dst2 = _row_block(buf2, t)
        return (pltpu.make_async_copy(_row_block(ys_ref, pos_ref[0, 2 * t]), dst1, sem),
                pltpu.make_async_copy(_row_block(ys_ref, pos_ref[0, 2 * t + 1]), dst2, sem))

    def start(t, carry):
        for k, cp in enumerate(copies(t)):
            cp.start(priority=k)
        return carry

    def wait(t, carry):
        for cp in copies(t):
            cp.wait()
        return carry

    lax.fori_loop(0, tmc, start, 0)
    lax.fori_loop(0, tmc, wait, 0)
    w1 = route_ref[:, R_W1:R_W1 + 1]
    w2 = route_ref[:, R_W2:R_W2 + 1]
    for k in range(out_ref.shape[1] // LANES):
        sl = slice(k * LANES, (k + 1) * LANES)
        y = w1 * buf1[pl.ds(k, tmc, stride=SUBLANES), :] + w2 * buf2[pl.ds(k, tmc, stride=SUBLANES), :]
        out_ref[:, sl] = x_ref[:, sl] + mod_ref[G2:G2 + 1, sl] * y


def _combine(L, ys, pos, route, x, mod, with_ctx):
    tmc, d = L.TC, L.D
    ratio = L.tm // tmc
    n_t = L.tiles(with_ctx) * ratio
    return pl.pallas_call(
        functools.partial(_combine_kernel, tmc=tmc),
        out_shape=jax.ShapeDtypeStruct((L.rows(with_ctx), d), F32),
        grid=(n_t,),
        in_specs=[pl.BlockSpec((None, 1, 2 * tmc), lambda i: (i, 0, 0), memory_space=pltpu.SMEM),
                  pl.BlockSpec(memory_space=pl.ANY),
                  pl.BlockSpec((tmc, LANES), lambda i: (i, 0)),
                  pl.BlockSpec((tmc, d), lambda i: (i, 0)),
                  pl.BlockSpec((None, SUBLANES, d), lambda i: (L.mod_group(i // ratio), 0, 0))],
        out_specs=pl.BlockSpec((tmc, d), lambda i: (i, 0)),
        scratch_shapes=[pltpu.VMEM((tmc * SUBLANES, LANES), F32), pltpu.VMEM((tmc * SUBLANES, LANES), F32),
                        pltpu.SemaphoreType.DMA(())],
        compiler_params=_params(("arbitrary",)),
        name="moe_combine",
    )(pos.reshape(-1, 1, 2 * tmc)[:n_t], ys, route, x, mod)


def _moe(L, x, mod, g, w_router, we_gu, we_down, with_ctx):
    tmx = L.tm
    n_tok = L.N if with_ctx else L.NL
    route, counts = _router(L, x, mod, g, w_router, with_ctx)
    cnt = counts[0, :N_EXPERTS].astype(jnp.int32)
    tiles_e = (cnt + tmx - 1) // tmx
    tile_end = jnp.cumsum(tiles_e)
    start = (tile_end - tiles_e) * tmx
    n_tiles = (2 * n_tok) // tmx + N_EXPERTS
    tile_id = jnp.arange(n_tiles, dtype=jnp.int32)
    tile_expert = jnp.minimum(jnp.sum(tile_id[:, None] >= tile_end[None, :], axis=1), N_EXPERTS - 1).astype(jnp.int32)
    first_tile = (tile_end - tiles_e)[tile_expert]
    tile_rows = jnp.clip(cnt[tile_expert] - (tile_id - first_tile) * tmx, 0, tmx)
    tile_rows = jnp.where(tile_id < tile_end[-1], tile_rows, 0).astype(jnp.int32)
    rt = route[:n_tok]
    sel = jnp.stack([rt[:, R_I1], rt[:, R_I2]], axis=1).astype(jnp.int32)
    rank = jnp.stack([rt[:, R_R1], rt[:, R_R2]], axis=1).astype(jnp.int32)
    pos = (jnp.sum(jnp.where(sel[..., None] == jnp.arange(N_EXPERTS), start, 0), axis=-1) + rank).reshape(-1)
    xs = _dispatch(L, x, mod, g, pos, n_tiles * tmx, with_ctx)
    ys = _experts(L, xs, tile_expert, tile_rows, we_gu, we_down, tmx)
    return _combine(L, ys, pos, route, x, mod, with_ctx)


def _rope_tables(L, lane_dim, lane_first, lane_freq, lane_is_col, half):
    t = jnp.arange(L.T, dtype=jnp.int32)
    rows = (t // GRID_W).astype(F32)
    cols = (t % GRID_W).astype(F32)
    inv_freq = jnp.exp(-math.log(ROPE_THETA) * jnp.arange(half, dtype=F32) / half)
    pos = jnp.where(lane_is_col[None, :], cols[:, None], rows[:, None])
    ang = pos * inv_freq[lane_freq][None, :]
    cos = jnp.where(lane_dim[None, :], jnp.cos(ang), 1.0)
    sin = jnp.where(lane_dim[None, :], jnp.sin(ang), 0.0)
    sin = jnp.where(lane_first[None, :], -sin, sin)
    ident = jnp.ones((L.tm, LANES), F32)
    return jnp.concatenate([cos, ident], axis=0), jnp.concatenate([sin, 0.0 * ident], axis=0)


def _gqa_lane_perm():
    quarter = HEAD_DIM // 4
    order = []
    for half in range(2):
        for head in range(2):
            for sec in range(2):
                base = head * HEAD_DIM + sec * (HEAD_DIM // 2) + half * quarter
                order += list(range(base, base + quarter))
    return jnp.array(order, dtype=jnp.int32)


def _permute_blocks(a, perm):
    lead, n = a.shape[:-1], a.shape[-1]
    return a.reshape(lead + (n // LANES, LANES))[..., perm].reshape(lead + (n,))


def _gqa_rope_tables(L):
    d = _gqa_lane_perm() % HEAD_DIM
    dd = d % (HEAD_DIM // 2)
    quarter = HEAD_DIM // 4
    return _rope_tables(L, d >= 0, dd < quarter, dd % quarter, d >= HEAD_DIM // 2, quarter)


def _mla_rope_tables(L):
    lane = jnp.arange(LANES)
    d = lane - NOPE_DIM
    in_rope = (d >= 0) & (d < ROPE_DIM)
    dd = d % (ROPE_DIM // 2)
    quarter = ROPE_DIM // 4
    return _rope_tables(L, in_rope, dd < quarter, dd % quarter, d >= ROPE_DIM // 2, quarter)


def _segment_matrix(bounds):
    lane = jnp.arange(LANES)
    m = jnp.zeros((LANES, LANES), F32)
    for lo, hi in bounds:
        inside = (lane >= lo) & (lane < hi)
        m = m + jnp.where(inside[:, None] & inside[None, :], 1.0 / (hi - lo), 0.0)
    return m.astype(BF16)


def _segment_matrix_by_id(ids):
    same = ids[:, None] == ids[None, :]
    return (same / jnp.sum(same, axis=1, keepdims=True)).astype(BF16)


def _natten_bias(rel_bias):
    n_h = rel_bias.shape[0]
    col = jnp.arange(GRID_W, dtype=jnp.int32)
    c0 = jnp.clip(col - WIN_C // 2, 0, GRID_W - WIN_C)
    col_ok = (col[None, :] >= c0[:, None]) & (col[None, :] < c0[:, None] + WIN_C)
    col_idx = jnp.clip(col[None, :] - col[:, None] + WIN_C - 1, 0, 2 * WIN_C - 2)
    full = jnp.where(col_ok[None, None], rel_bias[:, :, col_idx], NEG_INF) * LOG2E
    masked = jnp.full((n_h, GRID_W, GRID_W), NEG_INF * LOG2E, F32)
    half = WIN_R // 2
    kinds = []
    for kind in range(3):
        q_rows = []
        for j in range(NAT_ROWS):
            ru, wu = ((j, 0), (half + j, j), (WIN_R + j, half))[kind]
            blocks = [full[:, u - ru + WIN_R - 1] if wu <= u < wu + WIN_R else masked for u in range(NAT_SPAN)]
            q_rows.append(jnp.concatenate(blocks, axis=-1))
        kinds.append(jnp.concatenate(q_rows, axis=1))
    return jnp.stack(kinds, axis=1).astype(F32)


def _dup_heads(w, n_heads):
    k = w.shape[0]
    w = w.reshape(k, n_heads, HEAD_DIM)
    return jnp.concatenate([w, w], axis=-1).reshape(k, n_heads * LANES)


def _half_pad_heads(w, n_heads, second):
    k = w.shape[0]
    w = w.reshape(k, n_heads, HEAD_DIM)
    z = jnp.zeros_like(w)
    return jnp.concatenate([z, w] if second else [w, z], axis=-1).reshape(k, n_heads * LANES)


def _ones_row(n_heads, second):
    lane = jnp.arange(LANES)
    return jnp.tile((lane == ONES_LANE[1 if second else 0]).astype(F32), n_heads)


def _alternate_heads(first, second, n_heads):
    lead = first.shape[:-1]
    f = first.reshape(lead + (n_heads, LANES))
    g = second.reshape(lead + (n_heads, LANES))
    odd = (jnp.arange(n_heads) % 2 == 1)[:, None]
    return jnp.where(odd, g, f).reshape(lead + (n_heads * LANES,))


def _logit_bound(q_sq, k_sq, scale):
    return (jnp.sqrt(q_sq * k_sq) * (scale * LOG2E * BOUND_MARGIN)).reshape(1, 1).astype(F32)


def _gain_sq(gain):
    return gain.shape[0] * jnp.max(jnp.abs(gain)) ** 2


def _gqa_mixer(L, x, mod, g_mix, p, need_ctx, tables):
    w = p["w_qkv"]
    nq = A_HEADS * HEAD_DIM
    nkv = A_KV_HEADS * HEAD_DIM
    w_v = w[:, nq + nkv:]
    perm = _gqa_lane_perm()
    w_qk = jnp.concatenate([w[:, :nq], _dup_heads(w[:, nq:nq + nkv], A_KV_HEADS)], axis=1)
    w_p = jnp.concatenate([_permute_blocks(w_qk, perm), _half_pad_heads(w_v, A_KV_HEADS, False),
                           _half_pad_heads(w_v, A_KV_HEADS, True)], axis=1)
    n_blk = A_KV_HEADS * LANES
    scale = HEAD_DIM ** -0.5
    gain_qk = jnp.concatenate([jnp.tile(p["q_gain"], A_HEADS) * (scale * LOG2E), jnp.tile(p["k_gain"], 2 * A_KV_HEADS)])
    gain = jnp.concatenate([_permute_blocks(gain_qk, perm), jnp.ones((2 * n_blk,), F32)]).reshape(1, -1)
    colbias = jnp.concatenate([jnp.zeros((nq + n_blk,), F32), _ones_row(A_KV_HEADS, False),
                               _ones_row(A_KV_HEADS, True)]).reshape(1, -1)
    seg = _segment_matrix_by_id(perm // HEAD_DIM)
    tn = 512
    qkv = _proj(L, "gqa_qkv", x, w_p, tn=tn, mod=mod, g=g_mix, seg=seg, gain=gain, colbias=colbias,
                n_norm=(nq + n_blk) // tn, rope=(tables["gqa_cos"], tables["gqa_sin"], LANES // 2))
    q_blocks = nq // LANES
    k_blocks = n_blk // LANES
    rep_pairs = N_PAIRS // A_KV_HEADS
    bound = _logit_bound(_gain_sq(p["q_gain"]), _gain_sq(p["k_gain"]), scale)
    def attend(mode):
        return _attention(L, "gqa_attn_" + mode, bound, qkv, [lambda p_: p_], qkv,
                          [lambda p_: q_blocks + p_ // rep_pairs],
                          qkv, [lambda p_: q_blocks + k_blocks + p_ // rep_pairs,
                                lambda p_: q_blocks + 2 * k_blocks + p_ // rep_pairs],
                          pair=True, q_head_lanes=HEAD_DIM // 2, mode=mode)

    return (attend("lat"), attend("ctx")) if need_ctx else attend("lat")


def _natten_mixer(L, x, mod, g_mix, p, need_ctx):
    n = B_HEADS * HEAD_DIM
    w = p["w_qkv"]
    w_v = w[:, 2 * n:]
    w_p = jnp.concatenate([w[:, :2 * n], _alternate_heads(_half_pad_heads(w_v, B_HEADS, False),
                                                          _half_pad_heads(w_v, B_HEADS, True), B_HEADS)], axis=1)
    scale = HEAD_DIM ** -0.5
    n_v = B_HEADS * LANES
    gain = jnp.concatenate([jnp.tile(p["q_gain"], B_HEADS) * (scale * LOG2E), jnp.tile(p["k_gain"], B_HEADS),
                            jnp.ones((n_v,), F32)]).reshape(1, -1)
    colbias = jnp.concatenate([jnp.zeros((2 * n,), F32),
                               _alternate_heads(_ones_row(B_HEADS, False), _ones_row(B_HEADS, True), B_HEADS)]
                              ).reshape(1, -1)
    seg = _segment_matrix([(0, HEAD_DIM), (HEAD_DIM, LANES)])
    tn = 512
    qkv = _proj(L, "nat_qkv", x, w_p, tn=tn, mod=mod, g=g_mix, seg=seg, gain=gain, colbias=colbias,
                n_norm=2 * n // tn)
    qk_bound = _logit_bound(_gain_sq(p["q_gain"]), _gain_sq(p["k_gain"]), scale)
    bound = qk_bound + jnp.maximum(jnp.max(p["rel_bias"]), 0.0) * LOG2E
    o = _natten(L, bound, qkv, _natten_bias(p["rel_bias"]))
    if need_ctx:
        o_ctx = _attention(L, "nat_ctx_attn", qk_bound, qkv, [lambda p_: p_], qkv, [lambda p_: N_PAIRS + p_],
                           qkv, [lambda p_: 2 * N_PAIRS + 2 * p_, lambda p_: 2 * N_PAIRS + 2 * p_ + 1],
                           pair=True, mode="ctx")
        o = (o, o_ctx)
    return o


def _mla_mixer(L, x, mod, g_mix, p, need_ctx, tables):
    d = L.D
    w_in = p["w_in"]
    n_c = Q_LORA + KV_LORA
    w_in_p = jnp.concatenate([w_in[:, :n_c], jnp.zeros((d, NOPE_DIM), F32), w_in[:, n_c:],
                              jnp.zeros((d, LANES - NOPE_DIM - ROPE_DIM), F32)], axis=1)
    dq = NOPE_DIM + ROPE_DIM
    pad_q = jnp.zeros((LANES - dq,), F32)
    seg_rope = _segment_matrix([(NOPE_DIM, dq)])
    gain_kr = jnp.concatenate([jnp.zeros((NOPE_DIM,), F32), p["k_gain"][NOPE_DIM:], pad_q]).reshape(1, LANES)
    cos, sin = tables["mla_cos"], tables["mla_sin"]
    cq, ckv, kr = _mla_in(L, x, mod, g_mix, w_in_p, p["g_dq"], p["g_dkv"], seg_rope, gain_kr, cos, sin)

    scale = dq ** -0.5
    w_uq = jnp.pad(p["w_uq"].reshape(Q_LORA, C_HEADS, dq), ((0, 0), (0, 0), (0, LANES - dq)))
    gain_q = jnp.tile(jnp.concatenate([p["q_gain"], pad_q]) * (scale * LOG2E), C_HEADS).reshape(1, -1)
    seg_q = _segment_matrix([(0, NOPE_DIM), (NOPE_DIM, dq)])
    n_hl = C_HEADS * LANES
    tn = 512
    q = _proj(L, "mla_q", cq, w_uq.reshape(Q_LORA, n_hl), tn=tn, seg=seg_q, gain=gain_q, n_norm=n_hl // tn,
              rope=(cos, sin, ROPE_DIM // 4))

    w_ukv = p["w_ukv"].reshape(KV_LORA, C_HEADS, NOPE_DIM + V_DIM)
    w_uk = jnp.pad(w_ukv[:, :, :NOPE_DIM], ((0, 0), (0, 0), (0, LANES - NOPE_DIM))).reshape(KV_LORA, n_hl)
    w_v = w_ukv[:, :, NOPE_DIM:].reshape(KV_LORA, C_HEADS * V_DIM)
    w_uv = _alternate_heads(_half_pad_heads(w_v, C_HEADS, False), _half_pad_heads(w_v, C_HEADS, True), C_HEADS)
    gain_k = jnp.tile(jnp.concatenate([p["k_gain"][:NOPE_DIM], jnp.zeros((LANES - NOPE_DIM,), F32)]),
                      C_HEADS).reshape(1, -1)
    seg_k = _segment_matrix([(0, NOPE_DIM)])
    ones = _alternate_heads(_ones_row(C_HEADS, False), _ones_row(C_HEADS, True), C_HEADS)
    kv = _proj(L, "mla_kv", ckv, jnp.concatenate([w_uk, w_uv], axis=1), tn=tn, seg=seg_k,
               gain=jnp.concatenate([gain_k, jnp.ones((1, n_hl), F32)], axis=1), n_norm=n_hl // tn, add=kr,
               colbias=jnp.concatenate([jnp.zeros((n_hl,), F32), ones]).reshape(1, -1))
    q_sq = _gain_sq(p["q_gain"][:NOPE_DIM]) + _gain_sq(p["q_gain"][NOPE_DIM:])
    k_sq = _gain_sq(p["k_gain"][:NOPE_DIM]) + _gain_sq(p["k_gain"][NOPE_DIM:])
    bound = _logit_bound(q_sq, k_sq, scale)
    heads = [lambda p_: 2 * p_, lambda p_: 2 * p_ + 1]
    v_heads = [lambda p_: C_HEADS + 2 * p_, lambda p_: C_HEADS + 2 * p_ + 1]
    def attend(mode):
        return _attention(L, "mla_attn_" + mode, bound, q, heads, kv, heads, kv, v_heads, pair=False, mode=mode)

    return (attend("lat"), attend("ctx")) if need_ctx else attend("lat")


def kernel(x, c, ctx, c_ctx, l0_w_mod, l0_b_mod, l0_g_mix, l0_g_ffn, l0_w_qkv, l0_q_gain, l0_k_gain, l0_w_o, l0_w_gu, l0_w_down, l1_w_mod, l1_b_mod, l1_g_mix, l1_g_ffn, l1_w_qkv, l1_q_gain, l1_k_gain, l1_rel_bias, l1_w_o, l1_w_router, l1_we_gu, l1_we_down, l2_w_mod, l2_b_mod, l2_g_mix, l2_g_ffn, l2_w_in, l2_g_dq, l2_g_dkv, l2_w_uq, l2_w_ukv, l2_q_gain, l2_k_gain, l2_w_o, l2_w_gu, l2_w_down, l3_w_mod, l3_b_mod, l3_g_mix, l3_g_ffn, l3_w_qkv, l3_q_gain, l3_k_gain, l3_w_o, l3_w_router, l3_we_gu, l3_we_down):
    layers = [
        dict(w_mod=l0_w_mod, b_mod=l0_b_mod, g_mix=l0_g_mix, g_ffn=l0_g_ffn, kind="gqa",
             mix=dict(w_qkv=l0_w_qkv, q_gain=l0_q_gain, k_gain=l0_k_gain), w_o=l0_w_o,
             ffn=dict(w_gu=l0_w_gu, w_down=l0_w_down)),
        dict(w_mod=l1_w_mod, b_mod=l1_b_mod, g_mix=l1_g_mix, g_ffn=l1_g_ffn, kind="natten",
             mix=dict(w_qkv=l1_w_qkv, q_gain=l1_q_gain, k_gain=l1_k_gain, rel_bias=l1_rel_bias), w_o=l1_w_o,
             moe=dict(w_router=l1_w_router, we_gu=l1_we_gu, we_down=l1_we_down)),
        dict(w_mod=l2_w_mod, b_mod=l2_b_mod, g_mix=l2_g_mix, g_ffn=l2_g_ffn, kind="mla",
             mix=dict(w_in=l2_w_in, g_dq=l2_g_dq, g_dkv=l2_g_dkv, w_uq=l2_w_uq, w_ukv=l2_w_ukv,
                      q_gain=l2_q_gain, k_gain=l2_k_gain), w_o=l2_w_o,
             ffn=dict(w_gu=l2_w_gu, w_down=l2_w_down)),
        dict(w_mod=l3_w_mod, b_mod=l3_b_mod, g_mix=l3_g_mix, g_ffn=l3_g_ffn, kind="gqa",
             mix=dict(w_qkv=l3_w_qkv, q_gain=l3_q_gain, k_gain=l3_k_gain), w_o=l3_w_o,
             moe=dict(w_router=l3_w_router, we_gu=l3_we_gu, we_down=l3_we_down)),
    ]
    batch, seq, d_model = x.shape
    L = _Layout(batch, seq, ctx.shape[1], d_model)
    gqa_cos, gqa_sin = _gqa_rope_tables(L)
    mla_cos, mla_sin = _mla_rope_tables(L)
    tables = dict(gqa_cos=gqa_cos, gqa_sin=gqa_sin, mla_cos=mla_cos, mla_sin=mla_sin)
    cvec = jnp.concatenate([c, c_ctx[None, :], jnp.zeros((SUBLANES - batch - 1, d_model), F32)], axis=0)
    xs = (x.reshape(L.NL, d_model), ctx.reshape(L.NC, d_model))
    for li, p in enumerate(layers):
        need_ctx = li < len(layers) - 1
        mod = _adaln(cvec, p["w_mod"], p["b_mod"], batch + 1)
        if p["kind"] == "gqa":
            o = _gqa_mixer(L, xs, mod, p["g_mix"], p["mix"], need_ctx, tables)
        elif p["kind"] == "natten":
            o = _natten_mixer(L, xs, mod, p["g_mix"], p["mix"], need_ctx)
        else:
            o = _mla_mixer(L, xs, mod, p["g_mix"], p["mix"], need_ctx, tables)
        xs = _oproj(L, o, p["w_o"], xs, mod, need_ctx)
        if "ffn" in p:
            xs = _ffn(L, xs, mod, p["g_ffn"], p["ffn"]["w_gu"], p["ffn"]["w_down"], need_ctx)
        else:
            xs = _moe(L, xs, mod, p["g_ffn"], p["moe"]["w_router"], p["moe"]["we_gu"], p["moe"]["we_down"], need_ctx)
    return xs[:L.NL].reshape(batch, seq, d_model)
```
